```python
import jax, jax.numpy as jnp
from jax import lax
import numpy as np

D_MODEL = 1024
BATCH = 8
SEQ = 4096
DEPTH = 4

N_HEADS = 16
HEAD_DIM = D_MODEL // N_HEADS
N_KV_HEADS = 4
GROUP = N_HEADS // N_KV_HEADS
ROT_DIM = HEAD_DIM // 4
ROPE_THETA = 500000.0
WINDOW = 128
BLOCK = 128
CONV_WIDTH = D_MODEL
CONV_K = 3
D_FF = -(-8 * D_MODEL // (3 * 256)) * 256
Q_W = N_HEADS * HEAD_DIM
KV_W = N_KV_HEADS * HEAD_DIM
IN_COLS = Q_W + 2 * KV_W + 3 * CONV_WIDTH + 2 * D_MODEL
RMS_EPS = 1e-6
NEG_INF = -1e30

kernel_name = "hybrid_gated_swa_shortconv_encoder"


def rms_norm(x, g):
    xf = x.astype(jnp.float32)
    y = xf * lax.rsqrt(jnp.mean(xf * xf, axis=-1, keepdims=True) + RMS_EPS)
    return (y * g.astype(jnp.float32)).astype(x.dtype)


def rope_tables(seq):
    pos = jnp.arange(seq, dtype=jnp.float32)
    inv = jnp.power(ROPE_THETA, -jnp.arange(0, ROT_DIM, 2, dtype=jnp.float32) / ROT_DIM)
    ang = pos[:, None] * inv[None, :]
    return jnp.cos(ang), jnp.sin(ang)


def partial_rope(t, cos, sin):
    tf = t.astype(jnp.float32)
    half = ROT_DIM // 2
    x1, x2, rest = tf[..., :half], tf[..., half:ROT_DIM], tf[..., ROT_DIM:]
    c, s = cos[None, :, None, :], sin[None, :, None, :]
    out = jnp.concatenate([x1 * c - x2 * s, x2 * c + x1 * s, rest], axis=-1)
    return out.astype(t.dtype)


def band_attention(q, k, v, sink):
    b, s = q.shape[0], q.shape[1]
    nb = s // BLOCK
    qb = q.reshape(b, nb, BLOCK, N_KV_HEADS, GROUP, HEAD_DIM).transpose(1, 0, 2, 3, 4, 5)

    def windows(t):
        tp = jnp.pad(t, ((0, 0), (BLOCK, BLOCK), (0, 0), (0, 0)))
        tb = tp.reshape(b, nb + 2, BLOCK, N_KV_HEADS, HEAD_DIM)
        w = jnp.concatenate([tb[:, :-2], tb[:, 1:-1], tb[:, 2:]], axis=2)
        return w.transpose(1, 0, 2, 3, 4)

    kw, vw = windows(k), windows(v)
    blk = jnp.arange(nb)
    q_idx = blk[:, None] * BLOCK + jnp.arange(BLOCK)[None, :]
    k_idx = (blk[:, None] - 1) * BLOCK + jnp.arange(3 * BLOCK)[None, :]
    mask = ((jnp.abs(q_idx[:, :, None] - k_idx[:, None, :]) <= WINDOW)
            & (k_idx[:, None, :] >= 0) & (k_idx[:, None, :] < s))
    sink_f = sink.astype(jnp.float32).reshape(N_KV_HEADS, GROUP)[None, :, :, None, None]
    scale = HEAD_DIM ** -0.5

    def one_block(args):
        qi, ki, vi, mi = args
        sc = jnp.einsum('bqkgd,bskd->bkgqs', qi.astype(jnp.float32), ki.astype(jnp.float32)) * scale
        sc = jnp.where(mi[None, None, None], sc, NEG_INF)
        m = jnp.maximum(jnp.max(sc, axis=-1, keepdims=True), sink_f)
        p = jnp.exp(sc - m)
        denom = jnp.sum(p, axis=-1, keepdims=True) + jnp.exp(sink_f - m)
        o = jnp.einsum('bkgqs,bskd->bqkgd', p / denom, vi.astype(jnp.float32))
        return o.astype(qi.dtype)

    o = lax.map(one_block, (qb, kw, vw, mask))
    return o.transpose(1, 0, 2, 3, 4, 5).reshape(b, s, Q_W)


def short_conv(u, w):
    up = jnp.pad(u, ((0, 0), (1, 1), (0, 0)))
    return up[:, :-2] * w[0] + up[:, 1:-1] * w[1] + up[:, 2:] * w[2]


def setup_inputs(seed: int = 0) -> dict:
    key = jax.random.key(seed)
    ks = jax.random.split(key, 16)
    f32 = jnp.float32

    def nrm(k, shape, fan_in):
        return jax.random.normal(k, shape, f32) * (fan_in ** -0.5)

    def gain(k):
        return 1.0 + 0.05 * jax.random.normal(k, (DEPTH, D_MODEL), f32)

    return {
        "x": jax.random.normal(ks[0], (BATCH, SEQ, D_MODEL), f32),
        "g_pre_mix": gain(ks[1]),
        "w_in": nrm(ks[2], (DEPTH, D_MODEL, IN_COLS), D_MODEL),
        "attn_sink": 0.5 * jax.random.normal(ks[3], (DEPTH, N_HEADS), f32),
        "conv_w": nrm(ks[4], (DEPTH, CONV_K, CONV_WIDTH), CONV_K),
        "w_attn_proj": nrm(ks[5], (DEPTH, Q_W, D_MODEL), Q_W),
        "w_conv_proj": nrm(ks[6], (DEPTH, CONV_WIDTH, D_MODEL), CONV_WIDTH),
        "w_out": nrm(ks[7], (DEPTH, D_MODEL, D_MODEL), D_MODEL),
        "g_post_mix": gain(ks[8]),
        "g_pre_ffn": gain(ks[9]),
        "w_gate": nrm(ks[10], (DEPTH, D_MODEL, D_FF), D_MODEL),
        "w_up": nrm(ks[11], (DEPTH, D_MODEL, D_FF), D_MODEL),
        "w_down": nrm(ks[12], (DEPTH, D_FF, D_MODEL), D_FF),
        "g_post_ffn": gain(ks[13]),
    }


def reference(x, g_pre_mix, w_in, attn_sink, conv_w, w_attn_proj, w_conv_proj, w_out,
              g_post_mix, g_pre_ffn, w_gate, w_up, w_down, g_post_ffn):
    b, s, _ = x.shape
    cos, sin = rope_tables(s)
    splits = np.cumsum([Q_W, KV_W, KV_W, CONV_WIDTH, CONV_WIDTH, CONV_WIDTH, D_MODEL]).tolist()
    for l in range(DEPTH):
        h = rms_norm(x, g_pre_mix[l])
        z = h @ w_in[l]
        q, k, v, cb, cc, cx, ga, gb = jnp.split(z, splits, axis=-1)
        q = partial_rope(q.reshape(b, s, N_HEADS, HEAD_DIM), cos, sin)
        k = partial_rope(k.reshape(b, s, N_KV_HEADS, HEAD_DIM), cos, sin)
        v = v.reshape(b, s, N_KV_HEADS, HEAD_DIM)
        y_attn = band_attention(q, k, v, attn_sink[l]) @ w_attn_proj[l]
        y_conv = (cb * short_conv(cc * cx, conv_w[l])) @ w_conv_proj[l]
        mix = (jax.nn.sigmoid(ga) * y_attn + jax.nn.sigmoid(gb) * y_conv) @ w_out[l]
        x = x + rms_norm(mix, g_post_mix[l])
        h = rms_norm(x, g_pre_ffn[l])
        f = (jax.nn.silu(h @ w_gate[l]) * (h @ w_up[l])) @ w_down[l]
        x = x + rms_norm(f, g_post_ffn[l])
    return x
```

```python
import functools

import jax
import jax.numpy as jnp
from jax import lax
from jax.experimental import pallas as pl
from jax.experimental.pallas import tpu as pltpu

D_MODEL = 1024
N_HEADS = 16
HEAD_DIM = 64
N_KV_HEADS = 4
GROUP = N_HEADS // N_KV_HEADS
ROT_DIM = HEAD_DIM // 4
ROT_HALF = ROT_DIM // 2
ROPE_THETA = 500000.0
WINDOW = 128
BLOCK = 128
CONV_K = 3
D_FF = 2816
Q_W = N_HEADS * HEAD_DIM
KV_W = N_KV_HEADS * HEAD_DIM
IN_COLS = Q_W + 2 * KV_W + 3 * D_MODEL + 2 * D_MODEL
RMS_EPS = 1e-6
NEG_INF = -1e30

LANES = 128
SUBLANES = 8
TOKEN_TILE = 512
COL_CHUNK = 512
VMEM_LIMIT = 56 * 1024 * 1024

_OFF_Q = 0
_OFF_K = Q_W
_OFF_V = Q_W + KV_W
_OFF_CB = Q_W + 2 * KV_W
_OFF_CC = _OFF_CB + D_MODEL
_OFF_CX = _OFF_CC + D_MODEL
_OFF_GA = _OFF_CX + D_MODEL
_OFF_GB = _OFF_GA + D_MODEL


def _rms_norm_f32(x, g):
    return x * lax.rsqrt(jnp.mean(x * x, axis=-1, keepdims=True) + RMS_EPS) * g


def _rope_chunk(z, c, sa, sb):
    return z * c + pltpu.roll(z, LANES - ROT_HALF, axis=1) * sa + pltpu.roll(z, ROT_HALF, axis=1) * sb


def _in_proj_kernel(x_ref, g_ref, w_ref, c_ref, sa_ref, sb_ref,
                    q_ref, k_ref, v_ref, cb_ref, u_ref, sga_ref, sgb_ref):
    h = _rms_norm_f32(x_ref[...], g_ref[...]).astype(jnp.bfloat16)

    def proj(off, width):
        return jnp.dot(h, w_ref[:, off:off + width], preferred_element_type=jnp.float32)

    c, sa, sb = c_ref[...], sa_ref[...], sb_ref[...]
    scale = HEAD_DIM ** -0.5

    def rope(z):
        return jnp.concatenate(
            [_rope_chunk(z[:, j:j + LANES], c, sa, sb) for j in range(0, z.shape[1], LANES)], axis=1)

    for j in range(0, Q_W, COL_CHUNK):
        q_ref[:, j:j + COL_CHUNK] = (rope(proj(_OFF_Q + j, COL_CHUNK)) * scale).astype(q_ref.dtype)
    k_ref[...] = rope(proj(_OFF_K, KV_W)).astype(k_ref.dtype)
    v_ref[...] = proj(_OFF_V, KV_W).astype(v_ref.dtype)
    for j in range(0, D_MODEL, COL_CHUNK):
        cb_ref[:, j:j + COL_CHUNK] = proj(_OFF_CB + j, COL_CHUNK).astype(cb_ref.dtype)
        u = proj(_OFF_CC + j, COL_CHUNK) * proj(_OFF_CX + j, COL_CHUNK)
        u_ref[:, j:j + COL_CHUNK] = u.astype(u_ref.dtype)
        sga_ref[:, j:j + COL_CHUNK] = jax.nn.sigmoid(proj(_OFF_GA + j, COL_CHUNK)).astype(sga_ref.dtype)
        sgb_ref[:, j:j + COL_CHUNK] = jax.nn.sigmoid(proj(_OFF_GB + j, COL_CHUNK)).astype(sgb_ref.dtype)


def _in_proj(x, g, w_in, layer, tabs, seq):
    n = x.shape[0]
    tm = TOKEN_TILE
    pos_blocks = seq // tm
    row = lambda i: (i, 0)
    tab = lambda i: (i % pos_blocks, 0)
    bf = jnp.bfloat16
    out_shape = (
        jax.ShapeDtypeStruct((n, Q_W), bf), jax.ShapeDtypeStruct((n, KV_W), bf),
        jax.ShapeDtypeStruct((n, KV_W), bf), jax.ShapeDtypeStruct((n, D_MODEL), bf),
        jax.ShapeDtypeStruct((n, D_MODEL), bf), jax.ShapeDtypeStruct((n, D_MODEL), bf),
        jax.ShapeDtypeStruct((n, D_MODEL), bf))
    return pl.pallas_call(
        _in_proj_kernel,
        out_shape=out_shape,
        grid=(n // tm,),
        in_specs=[
            pl.BlockSpec((tm, D_MODEL), row),
            pl.BlockSpec((None, 1, D_MODEL), lambda i: (layer, 0, 0)),
            pl.BlockSpec((None, D_MODEL, IN_COLS), lambda i: (layer, 0, 0)),
            pl.BlockSpec((tm, LANES), tab), pl.BlockSpec((tm, LANES), tab), pl.BlockSpec((tm, LANES), tab),
        ],
        out_specs=(
            pl.BlockSpec((tm, Q_W), row), pl.BlockSpec((tm, KV_W), row), pl.BlockSpec((tm, KV_W), row),
            pl.BlockSpec((tm, D_MODEL), row), pl.BlockSpec((tm, D_MODEL), row),
            pl.BlockSpec((tm, D_MODEL), row), pl.BlockSpec((tm, D_MODEL), row)),
        compiler_params=pltpu.CompilerParams(
            dimension_semantics=("arbitrary",), vmem_limit_bytes=VMEM_LIMIT),
        name="in_proj",
    )(x, g, w_in, *tabs)


def _band_attn_kernel(sink_ref, q_ref, kp_ref, kc_ref, kn_ref, vp_ref, vc_ref, vn_ref, o_ref, *, layer, nb):
    i = pl.program_id(1)
    r = lax.broadcasted_iota(jnp.int32, (BLOCK, 3 * BLOCK), 0)
    c = lax.broadcasted_iota(jnp.int32, (BLOCK, 3 * BLOCK), 1)
    rel = c - BLOCK - r
    valid = (jnp.abs(rel) <= WINDOW)
    valid &= (c >= BLOCK) | (i > 0)
    valid &= (c < 2 * BLOCK) | (i < nb - 1)
    outs = []
    for g in range(N_KV_HEADS):
        ks = slice(g * HEAD_DIM, (g + 1) * HEAD_DIM)
        kw = jnp.concatenate([kp_ref[:, ks], kc_ref[:, ks], kn_ref[:, ks]], axis=0)
        vw = jnp.concatenate([vp_ref[:, ks], vc_ref[:, ks], vn_ref[:, ks]], axis=0)
        for hh in range(GROUP):
            head = g * GROUP + hh
            qh = q_ref[:, head * HEAD_DIM:(head + 1) * HEAD_DIM]
            s = lax.dot_general(qh, kw, (((1,), (1,)), ((), ())), preferred_element_type=jnp.float32)
            s = jnp.where(valid, s, NEG_INF)
            sink = sink_ref[layer, head]
            m = jnp.maximum(jnp.max(s, axis=-1, keepdims=True), sink)
            p = jnp.exp(s - m)
            denom = jnp.sum(p, axis=-1, keepdims=True) + jnp.exp(sink - m)
            o = jnp.dot(p.astype(jnp.bfloat16), vw, preferred_element_type=jnp.float32)
            outs.append(o / denom)
    o_ref[...] = jnp.concatenate(outs, axis=1).astype(o_ref.dtype)


def _band_attn(q, k, v, sink, layer, batch, seq):
    nb = seq // BLOCK
    cur = lambda b, i: (b * nb + i, 0)
    prev = lambda b, i: (b * nb + jnp.maximum(i - 1, 0), 0)
    nxt = lambda b, i: (b * nb + jnp.minimum(i + 1, nb - 1), 0)
    kv = lambda m: pl.BlockSpec((BLOCK, KV_W), m)
    return pl.pallas_call(
        functools.partial(_band_attn_kernel, layer=layer, nb=nb),
        out_shape=jax.ShapeDtypeStruct(q.shape, jnp.bfloat16),
        grid=(batch, nb),
        in_specs=[
            pl.BlockSpec(memory_space=pltpu.SMEM),
            pl.BlockSpec((BLOCK, Q_W), cur),
            kv(prev), kv(cur), kv(nxt), kv(prev), kv(cur), kv(nxt),
        ],
        out_specs=pl.BlockSpec((BLOCK, Q_W), cur),
        compiler_params=pltpu.CompilerParams(
            dimension_semantics=("arbitrary", "arbitrary"), vmem_limit_bytes=VMEM_LIMIT),
        name="band_attn",
    )(sink, q, k, k, k, v, v, v)


def _mix_tail_kernel(x_ref, a_ref, cb_ref, u_ref, up_ref, un_ref, sga_ref, sgb_ref,
                     cw_ref, wa_ref, wc_ref, wo_ref, g_ref, o_ref, *, pos_blocks):
    i = pl.program_id(0)
    tm = x_ref.shape[0]
    u = u_ref[...].astype(jnp.float32)
    first = (i % pos_blocks) == 0
    last = (i % pos_blocks) == pos_blocks - 1
    halo_prev = jnp.where(first, 0.0, up_ref[SUBLANES - 1:SUBLANES, :].astype(jnp.float32))
    halo_next = jnp.where(last, 0.0, un_ref[0:1, :].astype(jnp.float32))
    rows = lax.broadcasted_iota(jnp.int32, (tm, 1), 0)
    u_before = jnp.where(rows == 0, halo_prev, pltpu.roll(u, 1, axis=0))
    u_after = jnp.where(rows == tm - 1, halo_next, pltpu.roll(u, tm - 1, axis=0))
    cw = cw_ref[...]
    conv = u_before * cw[0:1, :] + u * cw[1:2, :] + u_after * cw[2:3, :]
    c = (cb_ref[...].astype(jnp.float32) * conv).astype(jnp.bfloat16)
    y_conv = jnp.dot(c, wc_ref[...], preferred_element_type=jnp.float32)
    y_attn = jnp.dot(a_ref[...], wa_ref[...], preferred_element_type=jnp.float32)
    mix = sga_ref[...].astype(jnp.float32) * y_attn + sgb_ref[...].astype(jnp.float32) * y_conv
    y = jnp.dot(mix.astype(jnp.bfloat16), wo_ref[...], preferred_element_type=jnp.float32)
    o_ref[...] = x_ref[...] + _rms_norm_f32(y, g_ref[...])


def _mix_tail(x, attn, cb, u, sga, sgb, conv_w, wa, wc, wo, g, layer, seq):
    n = x.shape[0]
    tm = TOKEN_TILE
    pos_blocks = seq // tm
    halo_blocks = tm // SUBLANES
    row = lambda i: (i, 0)
    prev = lambda i: (jnp.maximum(i * halo_blocks - 1, 0), 0)
    nxt = lambda i: (jnp.minimum((i + 1) * halo_blocks, n // SUBLANES - 1), 0)
    wspec = pl.BlockSpec((None, D_MODEL, D_MODEL), lambda i: (layer, 0, 0))
    tile = pl.BlockSpec((tm, D_MODEL), row)
    return pl.pallas_call(
        functools.partial(_mix_tail_kernel, pos_blocks=pos_blocks),
        out_shape=jax.ShapeDtypeStruct(x.shape, x.dtype),
        grid=(n // tm,),
        in_specs=[
            tile, tile, tile, tile,
            pl.BlockSpec((SUBLANES, D_MODEL), prev), pl.BlockSpec((SUBLANES, D_MODEL), nxt),
            tile, tile,
            pl.BlockSpec((None, CONV_K, D_MODEL), lambda i: (layer, 0, 0)),
            wspec, wspec, wspec,
            pl.BlockSpec((None, 1, D_MODEL), lambda i: (layer, 0, 0)),
        ],
        out_specs=tile,
        compiler_params=pltpu.CompilerParams(
            dimension_semantics=("arbitrary",), vmem_limit_bytes=VMEM_LIMIT),
        name="mix_tail",
    )(x, attn, cb, u, u, u, sga, sgb, conv_w, wa, wc, wo, g)


def _ffn_kernel(x_ref, gpre_ref, wg_ref, wu_ref, wd_ref, gpost_ref, o_ref, act_ref):
    x = x_ref[...]
    h = _rms_norm_f32(x, gpre_ref[...]).astype(jnp.bfloat16)
    for j in range(0, D_FF, COL_CHUNK):
        w = min(COL_CHUNK, D_FF - j)
        gate = jnp.dot(h, wg_ref[:, j:j + w], preferred_element_type=jnp.float32)
        up = jnp.dot(h, wu_ref[:, j:j + w], preferred_element_type=jnp.float32)
        act_ref[:, j:j + w] = (gate * jax.nn.sigmoid(gate) * up).astype(act_ref.dtype)
    f = jnp.dot(act_ref[...], wd_ref[...], preferred_element_type=jnp.float32)
    o_ref[...] = x + _rms_norm_f32(f, gpost_ref[...])


def _ffn(x, gpre, wg, wu, wd, gpost, layer):
    n = x.shape[0]
    tm = TOKEN_TILE
    tile = pl.BlockSpec((tm, D_MODEL), lambda i: (i, 0))
    gspec = pl.BlockSpec((None, 1, D_MODEL), lambda i: (layer, 0, 0))
    return pl.pallas_call(
        _ffn_kernel,
        out_shape=jax.ShapeDtypeStruct(x.shape, x.dtype),
        grid=(n // tm,),
        in_specs=[
            tile, gspec,
            pl.BlockSpec((None, D_MODEL, D_FF), lambda i: (layer, 0, 0)),
            pl.BlockSpec((None, D_MODEL, D_FF), lambda i: (layer, 0, 0)),
            pl.BlockSpec((None, D_FF, D_MODEL), lambda i: (layer, 0, 0)),
            gspec,
        ],
        out_specs=tile,
        scratch_shapes=[pltpu.VMEM((tm, D_FF), jnp.bfloat16)],
        compiler_params=pltpu.CompilerParams(
            dimension_semantics=("arbitrary",), vmem_limit_bytes=VMEM_LIMIT),
        name="ffn",
    )(x, gpre, wg, wu, wd, gpost)


def _rope_lane_tables(seq):
    pos = jnp.arange(seq, dtype=jnp.float32)
    inv = jnp.power(ROPE_THETA, -jnp.arange(0, ROT_DIM, 2, dtype=jnp.float32) / ROT_DIM)
    ang = pos[:, None] * inv[None, :]
    cos, sin = jnp.cos(ang), jnp.sin(ang)
    ones = jnp.ones((seq, HEAD_DIM - ROT_DIM), jnp.float32)
    zeros_r = jnp.zeros((seq, HEAD_DIM - ROT_DIM), jnp.float32)
    zeros_h = jnp.zeros((seq, ROT_HALF), jnp.float32)
    c = jnp.concatenate([cos, cos, ones], axis=1)
    sa = jnp.concatenate([-sin, zeros_h, zeros_r], axis=1)
    sb = jnp.concatenate([zeros_h, sin, zeros_r], axis=1)
    rep = LANES // HEAD_DIM
    return tuple(jnp.tile(t, (1, rep)) for t in (c, sa, sb))


def kernel(x, g_pre_mix, w_in, attn_sink, conv_w, w_attn_proj, w_conv_proj, w_out,
           g_post_mix, g_pre_ffn, w_gate, w_up, w_down, g_post_ffn):
    batch, seq, d = x.shape
    depth = w_in.shape[0]
    assert d == D_MODEL and seq % TOKEN_TILE == 0 and seq % BLOCK == 0
    bf = jnp.bfloat16
    w_in, w_attn_proj, w_conv_proj, w_out, w_gate, w_up, w_down = (
        w.astype(bf) for w in (w_in, w_attn_proj, w_conv_proj, w_out, w_gate, w_up, w_down))
    gains = [g.reshape(depth, 1, D_MODEL) for g in (g_pre_mix, g_post_mix, g_pre_ffn, g_post_ffn)]
    tabs = _rope_lane_tables(seq)
    xf = x.reshape(batch * seq, d)
    for layer in range(depth):
        q, k, v, cb, u, sga, sgb = _in_proj(xf, gains[0], w_in, layer, tabs, seq)
        attn = _band_attn(q, k, v, attn_sink, layer, batch, seq)
        xf = _mix_tail(xf, attn, cb, u, sga, sgb, conv_w, w_attn_proj, w_conv_proj, w_out,
                       gains[1], layer, seq)
        xf = _ffn(xf, gains[2], w_gate, w_up, w_down, gains[3], layer)
    return xf.reshape(batch, seq, d)
```

```python
import functools

import jax
import jax.numpy as jnp
from jax import lax
from jax.experimental import pallas as pl
from jax.experimental.pallas import tpu as pltpu

D_MODEL = 1024
N_HEADS = 16
HEAD_DIM = 64
N_KV_HEADS = 4
GROUP = N_HEADS // N_KV_HEADS
ROT_DIM = HEAD_DIM // 4
ROT_HALF = ROT_DIM // 2
ROPE_THETA = 500000.0
WINDOW = 128
BLOCK = 128
CONV_K = 3
D_FF = 2816
Q_W = N_HEADS * HEAD_DIM
KV_W = N_KV_HEADS * HEAD_DIM
IN_COLS = Q_W + 2 * KV_W + 3 * D_MODEL + 2 * D_MODEL
RMS_EPS = 1e-6
NEG_INF = -1e30

LANES = 128
SUBLANES = 8
TOKEN_TILE = 512
ATTN_TILE = 512
COL_CHUNK = 512
VMEM_LIMIT = 56 * 1024 * 1024

_OFF_Q = 0
_OFF_K = Q_W
_OFF_V = Q_W + KV_W
_OFF_CB = Q_W + 2 * KV_W
_OFF_CC = _OFF_CB + D_MODEL
_OFF_CX = _OFF_CC + D_MODEL
_OFF_GA = _OFF_CX + D_MODEL
_OFF_GB = _OFF_GA + D_MODEL


def _rms_norm_f32(x, g):
    return x * lax.rsqrt(jnp.mean(x * x, axis=-1, keepdims=True) + RMS_EPS) * g


def _rope_chunk(z, c, sa, sb):
    return z * c + pltpu.roll(z, LANES - ROT_HALF, axis=1) * sa + pltpu.roll(z, ROT_HALF, axis=1) * sb


def _in_proj_kernel(x_ref, g_ref, w_ref, wkt_ref, c_ref, sa_ref, sb_ref, ct_ref, st_ref,
                    q_ref, kt_ref, v_ref, cb_ref, u_ref, sga_ref, sgb_ref):
    h = _rms_norm_f32(x_ref[...], g_ref[...]).astype(jnp.bfloat16)

    def proj(off, width):
        return jnp.dot(h, w_ref[:, off:off + width], preferred_element_type=jnp.float32)

    c, sa, sb = c_ref[...], sa_ref[...], sb_ref[...]
    scale = HEAD_DIM ** -0.5

    def rope(z):
        return jnp.concatenate(
            [_rope_chunk(z[:, j:j + LANES], c, sa, sb) for j in range(0, z.shape[1], LANES)], axis=1)

    for j in range(0, Q_W, COL_CHUNK):
        q_ref[:, j:j + COL_CHUNK] = (rope(proj(_OFF_Q + j, COL_CHUNK)) * scale).astype(q_ref.dtype)

    kt = lax.dot_general(wkt_ref[...], h, (((1,), (1,)), ((), ())), preferred_element_type=jnp.float32)
    ct, st = ct_ref[...], st_ref[...]
    pieces = []
    for g in range(N_KV_HEADS):
        base = g * HEAD_DIM
        x1 = kt[base:base + ROT_HALF]
        x2 = kt[base + ROT_HALF:base + ROT_DIM]
        pieces += [x1 * ct - x2 * st, x2 * ct + x1 * st, kt[base + ROT_DIM:base + HEAD_DIM]]
    kt_ref[...] = jnp.concatenate(pieces, axis=0).astype(kt_ref.dtype)
    v_ref[...] = proj(_OFF_V, KV_W).astype(v_ref.dtype)
    for j in range(0, D_MODEL, COL_CHUNK):
        cb_ref[:, j:j + COL_CHUNK] = proj(_OFF_CB + j, COL_CHUNK).astype(cb_ref.dtype)
        u = proj(_OFF_CC + j, COL_CHUNK) * proj(_OFF_CX + j, COL_CHUNK)
        u_ref[:, j:j + COL_CHUNK] = u.astype(u_ref.dtype)
        sga_ref[:, j:j + COL_CHUNK] = jax.nn.sigmoid(proj(_OFF_GA + j, COL_CHUNK)).astype(sga_ref.dtype)
        sgb_ref[:, j:j + COL_CHUNK] = jax.nn.sigmoid(proj(_OFF_GB + j, COL_CHUNK)).astype(sgb_ref.dtype)


def _in_proj(x, g, w_in, w_kt, layer, tabs, tabs_t, seq):
    n = x.shape[0]
    tm = TOKEN_TILE
    pos_blocks = seq // tm
    row = lambda i: (i, 0)
    tab = lambda i: (i % pos_blocks, 0)
    tab_t = lambda i: (0, i % pos_blocks)
    bf = jnp.bfloat16
    out_shape = (
        jax.ShapeDtypeStruct((n, Q_W), bf), jax.ShapeDtypeStruct((KV_W, n), bf),
        jax.ShapeDtypeStruct((n, KV_W), bf), jax.ShapeDtypeStruct((n, D_MODEL), bf),
        jax.ShapeDtypeStruct((n, D_MODEL), bf), jax.ShapeDtypeStruct((n, D_MODEL), bf),
        jax.ShapeDtypeStruct((n, D_MODEL), bf))
    return pl.pallas_call(
        _in_proj_kernel,
        out_shape=out_shape,
        grid=(n // tm,),
        in_specs=[
            pl.BlockSpec((tm, D_MODEL), row),
            pl.BlockSpec((None, 1, D_MODEL), lambda i: (layer, 0, 0)),
            pl.BlockSpec((None, D_MODEL, IN_COLS), lambda i: (layer, 0, 0)),
            pl.BlockSpec((None, KV_W, D_MODEL), lambda i: (layer, 0, 0)),
            pl.BlockSpec((tm, LANES), tab), pl.BlockSpec((tm, LANES), tab), pl.BlockSpec((tm, LANES), tab),
            pl.BlockSpec((ROT_HALF, tm), tab_t), pl.BlockSpec((ROT_HALF, tm), tab_t),
        ],
        out_specs=(
            pl.BlockSpec((tm, Q_W), row), pl.BlockSpec((KV_W, tm), lambda i: (0, i)),
            pl.BlockSpec((tm, KV_W), row),
            pl.BlockSpec((tm, D_MODEL), row), pl.BlockSpec((tm, D_MODEL), row),
            pl.BlockSpec((tm, D_MODEL), row), pl.BlockSpec((tm, D_MODEL), row)),
        compiler_params=pltpu.CompilerParams(
            dimension_semantics=("arbitrary",), vmem_limit_bytes=VMEM_LIMIT),
        name="in_proj",
    )(x, g, w_in, w_kt, *tabs, *tabs_t)


def _band_attn_kernel(sink_ref, q_ref, ktl_ref, ktm_ref, ktr_ref, vl_ref, vm_ref, vr_ref, o_ref,
                      *, layer, n_tiles):
    i = pl.program_id(1)
    qb = q_ref.shape[0] // BLOCK
    r = lax.broadcasted_iota(jnp.int32, (BLOCK, BLOCK), 0)
    c = lax.broadcasted_iota(jnp.int32, (BLOCK, BLOCK), 1)
    left_ok = c >= r
    right_ok = c <= r
    left_edge_ok = left_ok & (i > 0)
    right_edge_ok = right_ok & (i < n_tiles - 1)
    zk = jnp.zeros((HEAD_DIM, BLOCK), jnp.bfloat16)
    zv = jnp.zeros((BLOCK, HEAD_DIM), jnp.bfloat16)
    zvv = jnp.zeros((BLOCK, 2 * HEAD_DIM), jnp.bfloat16)

    for g in range(N_KV_HEADS):
        rows = slice(g * HEAD_DIM, (g + 1) * HEAD_DIM)
        kz, vz = [], []
        for cb in range(qb + 2):
            if cb == 0:
                kt, vv = ktl_ref[rows, :], vl_ref[:, rows]
            elif cb == qb + 1:
                kt, vv = ktr_ref[rows, :], vr_ref[:, rows]
            else:
                cols = slice((cb - 1) * BLOCK, cb * BLOCK)
                kt, vv = ktm_ref[rows, cols], vm_ref[cols, rows]
            kz.append(jnp.concatenate(
                [jnp.concatenate([kt if a == h else zk for a in range(GROUP)], axis=1) for h in range(GROUP)],
                axis=0))
            lo = jnp.concatenate([vv, zv], axis=1)
            hi = jnp.concatenate([zv, vv], axis=1)
            vz.append(jnp.concatenate(
                [jnp.concatenate([lo, zvv], axis=1), jnp.concatenate([hi, zvv], axis=1),
                 jnp.concatenate([zvv, lo], axis=1), jnp.concatenate([zvv, hi], axis=1)], axis=0))
        for j in range(qb):
            q4 = q_ref[j * BLOCK:(j + 1) * BLOCK, g * GROUP * HEAD_DIM:(g + 1) * GROUP * HEAD_DIM]
            s = jnp.dot(q4, jnp.concatenate(kz[j:j + 3], axis=1), preferred_element_type=jnp.float32)
            lmask = left_edge_ok if j == 0 else left_ok
            rmask = right_edge_ok if j == qb - 1 else right_ok
            p = [[None] * GROUP for _ in range(3)]
            inv = []
            for h in range(GROUP):
                col = lambda w: slice((w * GROUP + h) * BLOCK, (w * GROUP + h + 1) * BLOCK)
                s0 = jnp.where(lmask, s[:, col(0)], NEG_INF)
                s1 = s[:, col(1)]
                s2 = jnp.where(rmask, s[:, col(2)], NEG_INF)
                sink = sink_ref[layer, g * GROUP + h]
                m = jnp.maximum(jnp.max(jnp.maximum(jnp.maximum(s0, s1), s2), axis=-1, keepdims=True), sink)
                p0, p1, p2 = jnp.exp(s0 - m), jnp.exp(s1 - m), jnp.exp(s2 - m)
                denom = jnp.sum(p0 + p1 + p2, axis=-1, keepdims=True) + jnp.exp(sink - m)
                p[0][h], p[1][h], p[2][h] = p0, p1, p2
                inv.append(jnp.broadcast_to(1.0 / denom, (BLOCK, HEAD_DIM)))
            p_all = jnp.concatenate([p[w][h] for w in range(3) for h in range(GROUP)], axis=1)
            o = jnp.dot(p_all.astype(jnp.bfloat16), jnp.concatenate(vz[j:j + 3], axis=0),
                        preferred_element_type=jnp.float32)
            o = o * jnp.concatenate(inv, axis=1)
            o_ref[j * BLOCK:(j + 1) * BLOCK, g * GROUP * HEAD_DIM:(g + 1) * GROUP * HEAD_DIM] = o.astype(o_ref.dtype)


def _band_attn(q, kt, v, sink, layer, batch, seq):
    tq = ATTN_TILE
    n_tiles = seq // tq
    qb = tq // BLOCK
    nb = seq // BLOCK
    left = lambda b, i: b * nb + jnp.maximum(i * qb - 1, 0)
    right = lambda b, i: b * nb + jnp.minimum((i + 1) * qb, nb - 1)
    return pl.pallas_call(
        functools.partial(_band_attn_kernel, layer=layer, n_tiles=n_tiles),
        out_shape=jax.ShapeDtypeStruct(q.shape, jnp.bfloat16),
        grid=(batch, n_tiles),
        in_specs=[
            pl.BlockSpec(memory_space=pltpu.SMEM),
            pl.BlockSpec((tq, Q_W), lambda b, i: (b * n_tiles + i, 0)),
            pl.BlockSpec((KV_W, BLOCK), lambda b, i: (0, left(b, i))),
            pl.BlockSpec((KV_W, tq), lambda b, i: (0, b * n_tiles + i)),
            pl.BlockSpec((KV_W, BLOCK), lambda b, i: (0, right(b, i))),
            pl.BlockSpec((BLOCK, KV_W), lambda b, i: (left(b, i), 0)),
            pl.BlockSpec((tq, KV_W), lambda b, i: (b * n_tiles + i, 0)),
            pl.BlockSpec((BLOCK, KV_W), lambda b, i: (right(b, i), 0)),
        ],
        out_specs=pl.BlockSpec((tq, Q_W), lambda b, i: (b * n_tiles + i, 0)),
        compiler_params=pltpu.CompilerParams(
            dimension_semantics=("arbitrary", "arbitrary"), vmem_limit_bytes=VMEM_LIMIT),
        name="band_attn",
    )(sink, q, kt, kt, kt, v, v, v)


def _mix_tail_kernel(x_ref, a_ref, cb_ref, u_ref, up_ref, un_ref, sga_ref, sgb_ref,
                     cw_ref, wa_ref, wc_ref, wo_ref, g_ref, o_ref, *, pos_blocks):
    i = pl.program_id(0)
    tm = x_ref.shape[0]
    u = u_ref[...].astype(jnp.float32)
    first = (i % pos_blocks) == 0
    last = (i % pos_blocks) == pos_blocks - 1
    halo_prev = jnp.where(first, 0.0, up_ref[SUBLANES - 1:SUBLANES, :].astype(jnp.float32))
    halo_next = jnp.where(last, 0.0, un_ref[0:1, :].astype(jnp.float32))
    rows = lax.broadcasted_iota(jnp.int32, (tm, 1), 0)
    u_before = jnp.where(rows == 0, halo_prev, pltpu.roll(u, 1, axis=0))
    u_after = jnp.where(rows == tm - 1, halo_next, pltpu.roll(u, tm - 1, axis=0))
    cw = cw_ref[...]
    conv = u_before * cw[0:1, :] + u * cw[1:2, :] + u_after * cw[2:3, :]
    c = (cb_ref[...].astype(jnp.float32) * conv).astype(jnp.bfloat16)
    y_conv = jnp.dot(c, wc_ref[...], preferred_element_type=jnp.float32)
    y_attn = jnp.dot(a_ref[...], wa_ref[...], preferred_element_type=jnp.float32)
    mix = sga_ref[...].astype(jnp.float32) * y_attn + sgb_ref[...].astype(jnp.float32) * y_conv
    y = jnp.dot(mix.astype(jnp.bfloat16), wo_ref[...], preferred_element_type=jnp.float32)
    o_ref[...] = x_ref[...] + _rms_norm_f32(y, g_ref[...])


def _mix_tail(x, attn, cb, u, sga, sgb, conv_w, wa, wc, wo, g, layer, seq):
    n = x.shape[0]
    tm = TOKEN_TILE
    pos_blocks = seq // tm
    halo_blocks = tm // SUBLANES
    row = lambda i: (i, 0)
    prev = lambda i: (jnp.maximum(i * halo_blocks - 1, 0), 0)
    nxt = lambda i: (jnp.minimum((i + 1) * halo_blocks, n // SUBLANES - 1), 0)
    wspec = pl.BlockSpec((None, D_MODEL, D_MODEL), lambda i: (layer, 0, 0))
    tile = pl.BlockSpec((tm, D_MODEL), row)
    return pl.pallas_call(
        functools.partial(_mix_tail_kernel, pos_blocks=pos_blocks),
        out_shape=jax.ShapeDtypeStruct(x.shape, x.dtype),
        grid=(n // tm,),
        in_specs=[
            tile, tile, tile, tile,
            pl.BlockSpec((SUBLANES, D_MODEL), prev), pl.BlockSpec((SUBLANES, D_MODEL), nxt),
            tile, tile,
            pl.BlockSpec((None, CONV_K, D_MODEL), lambda i: (layer, 0, 0)),
            wspec, wspec, wspec,
            pl.BlockSpec((None, 1, D_MODEL), lambda i: (layer, 0, 0)),
        ],
        out_specs=tile,
        compiler_params=pltpu.CompilerParams(
            dimension_semantics=("arbitrary",), vmem_limit_bytes=VMEM_LIMIT),
        name="mix_tail",
    )(x, attn, cb, u, u, u, sga, sgb, conv_w, wa, wc, wo, g)


def _ffn_kernel(x_ref, gpre_ref, wg_ref, wu_ref, wd_ref, gpost_ref, o_ref, act_ref):
    x = x_ref[...]
    h = _rms_norm_f32(x, gpre_ref[...]).astype(jnp.bfloat16)
    for j in range(0, D_FF, COL_CHUNK):
        w = min(COL_CHUNK, D_FF - j)
        gate = jnp.dot(h, wg_ref[:, j:j + w], preferred_element_type=jnp.float32)
        up = jnp.dot(h, wu_ref[:, j:j + w], preferred_element_type=jnp.float32)
        act_ref[:, j:j + w] = (gate * jax.nn.sigmoid(gate) * up).astype(act_ref.dtype)
    f = jnp.dot(act_ref[...], wd_ref[...], preferred_element_type=jnp.float32)
    o_ref[...] = x + _rms_norm_f32(f, gpost_ref[...])


def _ffn(x, gpre, wg, wu, wd, gpost, layer):
    n = x.shape[0]
    tm = TOKEN_TILE
    tile = pl.BlockSpec((tm, D_MODEL), lambda i: (i, 0))
    gspec = pl.BlockSpec((None, 1, D_MODEL), lambda i: (layer, 0, 0))
    return pl.pallas_call(
        _ffn_kernel,
        out_shape=jax.ShapeDtypeStruct(x.shape, x.dtype),
        grid=(n // tm,),
        in_specs=[
            tile, gspec,
            pl.BlockSpec((None, D_MODEL, D_FF), lambda i: (layer, 0, 0)),
            pl.BlockSpec((None, D_MODEL, D_FF), lambda i: (layer, 0, 0)),
            pl.BlockSpec((None, D_FF, D_MODEL), lambda i: (layer, 0, 0)),
            gspec,
        ],
        out_specs=tile,
        scratch_shapes=[pltpu.VMEM((tm, D_FF), jnp.bfloat16)],
        compiler_params=pltpu.CompilerParams(
            dimension_semantics=("arbitrary",), vmem_limit_bytes=VMEM_LIMIT),
        name="ffn",
    )(x, gpre, wg, wu, wd, gpost)


def _rope_lane_tables(seq):
    pos = jnp.arange(seq, dtype=jnp.float32)
    inv = jnp.power(ROPE_THETA, -jnp.arange(0, ROT_DIM, 2, dtype=jnp.float32) / ROT_DIM)
    ang = pos[:, None] * inv[None, :]
    cos, sin = jnp.cos(ang), jnp.sin(ang)
    ones = jnp.ones((seq, HEAD_DIM - ROT_DIM), jnp.float32)
    zeros_r = jnp.zeros((seq, HEAD_DIM - ROT_DIM), jnp.float32)
    zeros_h = jnp.zeros((seq, ROT_HALF), jnp.float32)
    c = jnp.concatenate([cos, cos, ones], axis=1)
    sa = jnp.concatenate([-sin, zeros_h, zeros_r], axis=1)
    sb = jnp.concatenate([zeros_h, sin, zeros_r], axis=1)
    rep = LANES // HEAD_DIM
    return tuple(jnp.tile(t, (1, rep)) for t in (c, sa, sb)), (cos.T, sin.T)


def kernel(x, g_pre_mix, w_in, attn_sink, conv_w, w_attn_proj, w_conv_proj, w_out,
           g_post_mix, g_pre_ffn, w_gate, w_up, w_down, g_post_ffn):
    batch, seq, d = x.shape
    depth = w_in.shape[0]
    assert d == D_MODEL and seq % TOKEN_TILE == 0 and seq % ATTN_TILE == 0 and ATTN_TILE % BLOCK == 0
    bf = jnp.bfloat16
    w_kt = jnp.swapaxes(w_in[:, :, _OFF_K:_OFF_K + KV_W], 1, 2).astype(bf)
    w_in, w_attn_proj, w_conv_proj, w_out, w_gate, w_up, w_down = (
        w.astype(bf) for w in (w_in, w_attn_proj, w_conv_proj, w_out, w_gate, w_up, w_down))
    gains = [g.reshape(depth, 1, D_MODEL) for g in (g_pre_mix, g_post_mix, g_pre_ffn, g_post_ffn)]
    tabs, tabs_t = _rope_lane_tables(seq)
    xf = x.reshape(batch * seq, d)
    for layer in range(depth):
        q, kt, v, cb, u, sga, sgb = _in_proj(xf, gains[0], w_in, w_kt, layer, tabs, tabs_t, seq)
        attn = _band_attn(q, kt, v, attn_sink, layer, batch, seq)
        xf = _mix_tail(xf, attn, cb, u, sga, sgb, conv_w, w_attn_proj, w_conv_proj, w_out,
                       gains[1], layer, seq)
        xf = _ffn(xf, gains[2], w_gate, w_up, w_down, gains[3], layer)
    return xf.reshape(batch, seq, d)
```

```python
import functools

import jax
import jax.numpy as jnp
from jax import lax
from jax.experimental import pallas as pl
from jax.experimental.pallas import tpu as pltpu

D_MODEL = 1024
N_HEADS = 16
HEAD_DIM = 64
N_KV_HEADS = 4
GROUP = N_HEADS // N_KV_HEADS
ROT_DIM = HEAD_DIM // 4
ROT_HALF = ROT_DIM // 2
ROPE_THETA = 500000.0
WINDOW = 128
BLOCK = 128
CONV_K = 3
D_FF = 2816
Q_W = N_HEADS * HEAD_DIM
KV_W = N_KV_HEADS * HEAD_DIM
GROUP_W = GROUP * HEAD_DIM
IN_COLS = Q_W + 2 * KV_W + 3 * D_MODEL + 2 * D_MODEL
RMS_EPS = 1e-6
NEG_INF = -1e30
LOG2E = 1.4426950408889634

LANES = 128
SUBLANES = 8
TOKEN_TILE = 512
COL_CHUNK = 512
SLAB = 16
UNITS_PER_ITER = 2
VMEM_LIMIT = 56 * 1024 * 1024

_OFF_Q = 0
_OFF_K = Q_W
_OFF_V = Q_W + KV_W
_OFF_CB = Q_W + 2 * KV_W
_OFF_CC = _OFF_CB + D_MODEL
_OFF_CX = _OFF_CC + D_MODEL
_OFF_GA = _OFF_CX + D_MODEL
_OFF_GB = _OFF_GA + D_MODEL


def _rms_norm_f32(x, g):
    return x * lax.rsqrt(jnp.mean(x * x, axis=-1, keepdims=True) + RMS_EPS) * g


def _rope_chunk(z, c, sa, sb):
    return z * c + pltpu.roll(z, LANES - ROT_HALF, axis=1) * sa + pltpu.roll(z, ROT_HALF, axis=1) * sb


def _in_proj_kernel(x_ref, g_ref, w_ref, wkt_ref, c_ref, sa_ref, sb_ref, ct_ref, st_ref,
                    q_ref, kt_ref, vz_ref, cb_ref, u_ref, sga_ref, sgb_ref):
    h = _rms_norm_f32(x_ref[...], g_ref[...]).astype(jnp.bfloat16)

    def proj(off, width):
        return jnp.dot(h, w_ref[:, off:off + width], preferred_element_type=jnp.float32)

    c, sa, sb = c_ref[...], sa_ref[...], sb_ref[...]
    scale = HEAD_DIM ** -0.5 * LOG2E

    def rope(z):
        return jnp.concatenate(
            [_rope_chunk(z[:, j:j + LANES], c, sa, sb) for j in range(0, z.shape[1], LANES)], axis=1)

    for j in range(0, Q_W, COL_CHUNK):
        q_ref[:, j:j + COL_CHUNK] = (rope(proj(_OFF_Q + j, COL_CHUNK)) * scale).astype(q_ref.dtype)

    kt = lax.dot_general(wkt_ref[...], h, (((1,), (1,)), ((), ())), preferred_element_type=jnp.float32)
    ct, st = ct_ref[...], st_ref[...]
    pieces = []
    for g in range(N_KV_HEADS):
        base = g * HEAD_DIM
        x1 = kt[base:base + ROT_HALF]
        x2 = kt[base + ROT_HALF:base + ROT_DIM]
        pieces += [x1 * ct - x2 * st, x2 * ct + x1 * st, kt[base + ROT_DIM:base + HEAD_DIM]]
    kt_ref[...] = jnp.concatenate(pieces, axis=0).astype(kt_ref.dtype)

    v = proj(_OFF_V, KV_W)
    zero = jnp.zeros((v.shape[0], HEAD_DIM), jnp.float32)
    halves = []
    for g in range(N_KV_HEADS):
        vg = v[:, g * HEAD_DIM:(g + 1) * HEAD_DIM]
        halves += [vg, zero, zero, vg]
    vz_ref[...] = jnp.concatenate(halves, axis=1).astype(vz_ref.dtype)

    for j in range(0, D_MODEL, COL_CHUNK):
        cb_ref[:, j:j + COL_CHUNK] = proj(_OFF_CB + j, COL_CHUNK).astype(cb_ref.dtype)
        u = proj(_OFF_CC + j, COL_CHUNK) * proj(_OFF_CX + j, COL_CHUNK)
        u_ref[:, j:j + COL_CHUNK] = u.astype(u_ref.dtype)
        sga_ref[:, j:j + COL_CHUNK] = jax.nn.sigmoid(proj(_OFF_GA + j, COL_CHUNK)).astype(sga_ref.dtype)
        sgb_ref[:, j:j + COL_CHUNK] = jax.nn.sigmoid(proj(_OFF_GB + j, COL_CHUNK)).astype(sgb_ref.dtype)


def _in_proj(x, g, w_in, w_kt, layer, tabs, tabs_t, seq):
    n = x.shape[0]
    tm = TOKEN_TILE
    pos_blocks = seq // tm
    row = lambda i: (i, 0)
    tab = lambda i: (i % pos_blocks, 0)
    tab_t = lambda i: (0, i % pos_blocks)
    bf = jnp.bfloat16
    out_shape = (
        jax.ShapeDtypeStruct((n, Q_W), bf), jax.ShapeDtypeStruct((KV_W, n), bf),
        jax.ShapeDtypeStruct((n, N_KV_HEADS * GROUP_W), bf), jax.ShapeDtypeStruct((n, D_MODEL), bf),
        jax.ShapeDtypeStruct((n, D_MODEL), bf), jax.ShapeDtypeStruct((n, D_MODEL), bf),
        jax.ShapeDtypeStruct((n, D_MODEL), bf))
    return pl.pallas_call(
        _in_proj_kernel,
        out_shape=out_shape,
        grid=(n // tm,),
        in_specs=[
            pl.BlockSpec((tm, D_MODEL), row),
            pl.BlockSpec((None, 1, D_MODEL), lambda i: (layer, 0, 0)),
            pl.BlockSpec((None, D_MODEL, IN_COLS), lambda i: (layer, 0, 0)),
            pl.BlockSpec((None, KV_W, D_MODEL), lambda i: (layer, 0, 0)),
            pl.BlockSpec((tm, LANES), tab), pl.BlockSpec((tm, LANES), tab), pl.BlockSpec((tm, LANES), tab),
            pl.BlockSpec((ROT_HALF, tm), tab_t), pl.BlockSpec((ROT_HALF, tm), tab_t),
        ],
        out_specs=(
            pl.BlockSpec((tm, Q_W), row), pl.BlockSpec((KV_W, tm), lambda i: (0, i)),
            pl.BlockSpec((tm, N_KV_HEADS * GROUP_W), row),
            pl.BlockSpec((tm, D_MODEL), row), pl.BlockSpec((tm, D_MODEL), row),
            pl.BlockSpec((tm, D_MODEL), row), pl.BlockSpec((tm, D_MODEL), row)),
        compiler_params=pltpu.CompilerParams(
            dimension_semantics=("arbitrary",), vmem_limit_bytes=VMEM_LIMIT),
        name="in_proj",
    )(x, g, w_in, w_kt, *tabs, *tabs_t)


def _band_attn_kernel(sink_ref, q_ref, kt_ref, vz_ref, o_ref,
                      s0_ref, s1_ref, p0_ref, p1_ref, inv0_ref, inv1_ref, *, layer):
    g = pl.program_id(1)
    nb = q_ref.shape[0] // BLOCK
    n_slabs = BLOCK // SLAB
    r = lax.broadcasted_iota(jnp.int32, (SLAB, BLOCK), 0)
    c = lax.broadcasted_iota(jnp.int32, (SLAB, BLOCK), 1)
    left_ok = [c >= r + t * SLAB for t in range(n_slabs)]
    right_ok = [c <= r + t * SLAB for t in range(n_slabs)]
    head_slot = lax.broadcasted_iota(jnp.int32, (SLAB, GROUP_W), 1) // HEAD_DIM
    sinks = [sink_ref[layer, g * GROUP + h] * LOG2E for h in range(GROUP)]
    zk = jnp.zeros((HEAD_DIM, BLOCK), jnp.bfloat16)
    zv = jnp.zeros((BLOCK, LANES), jnp.bfloat16)

    def rows_of(u):
        return pl.ds(pl.multiple_of(u * BLOCK, BLOCK), BLOCK)

    def window(u):
        return (jnp.maximum(u - 1, 0), u, jnp.minimum(u + 1, nb - 1))

    def scores(u, s_ref):
        tiles = []
        for cb in window(u):
            kt = kt_ref[:, rows_of(cb)]
            tiles.append(jnp.concatenate(
                [jnp.concatenate([kt if a == h else zk for a in range(GROUP)], axis=1) for h in range(GROUP)],
                axis=0))
        s_ref[...] = jnp.dot(q_ref[rows_of(u), :], jnp.concatenate(tiles, axis=1),
                             preferred_element_type=jnp.float32)

    def softmax(u, s_ref, p_ref, inv_ref):
        has_left, has_right = u > 0, u < nb - 1
        for t in range(n_slabs):
            rs = slice(t * SLAB, (t + 1) * SLAB)
            lmask = left_ok[t] & has_left
            rmask = right_ok[t] & has_right
            p = [[None] * GROUP for _ in range(3)]
            inv = []
            for h in range(GROUP):
                col = lambda w: slice((w * GROUP + h) * BLOCK, (w * GROUP + h + 1) * BLOCK)
                a0 = jnp.where(lmask, s_ref[rs, col(0)], NEG_INF)
                a1 = s_ref[rs, col(1)]
                a2 = jnp.where(rmask, s_ref[rs, col(2)], NEG_INF)
                m = jnp.maximum(jnp.max(jnp.maximum(jnp.maximum(a0, a1), a2), axis=-1, keepdims=True), sinks[h])
                e0, e1, e2 = jnp.exp2(a0 - m), jnp.exp2(a1 - m), jnp.exp2(a2 - m)
                denom = jnp.sum(e0 + e1 + e2, axis=-1, keepdims=True) + jnp.exp2(sinks[h] - m)
                p[0][h], p[1][h], p[2][h] = e0, e1, e2
                inv.append(1.0 / denom)
            p_ref[rs, :] = jnp.concatenate(
                [p[w][h] for w in range(3) for h in range(GROUP)], axis=1).astype(p_ref.dtype)
            inv_ref[rs, :] = jnp.where(
                head_slot == 0, inv[0], jnp.where(head_slot == 1, inv[1], jnp.where(head_slot == 2, inv[2], inv[3])))

    def values(u, p_ref, inv_ref):
        tiles = []
        for cb in window(u):
            even = vz_ref[rows_of(cb), 0:LANES]
            odd = vz_ref[rows_of(cb), LANES:2 * LANES]
            tiles += [jnp.concatenate([even, zv], axis=1), jnp.concatenate([odd, zv], axis=1),
                      jnp.concatenate([zv, even], axis=1), jnp.concatenate([zv, odd], axis=1)]
        o = jnp.dot(p_ref[...], jnp.concatenate(tiles, axis=0), preferred_element_type=jnp.float32)
        o_ref[rows_of(u), :] = (o * inv_ref[...]).astype(o_ref.dtype)

    s_refs, p_refs, inv_refs = (s0_ref, s1_ref), (p0_ref, p1_ref), (inv0_ref, inv1_ref)
    scores(0, s0_ref)
    p1_ref[...] = jnp.zeros_like(p1_ref)
    inv1_ref[...] = jnp.zeros_like(inv1_ref)

    def body(t, carry):
        for k in range(UNITS_PER_ITER):
            u = t * UNITS_PER_ITER + k
            cur, other = k % 2, (k + 1) % 2
            values(jnp.maximum(u - 1, 0), p_refs[other], inv_refs[other])
            scores(jnp.minimum(u + 1, nb - 1), s_refs[other])
            softmax(u, s_refs[cur], p_refs[cur], inv_refs[cur])
        return carry

    lax.fori_loop(0, nb // UNITS_PER_ITER, body, 0)
    values(nb - 1, p1_ref, inv1_ref)


def _band_attn(q, kt, vz, sink, layer, batch, seq):
    assert UNITS_PER_ITER % 2 == 0 and (seq // BLOCK) % UNITS_PER_ITER == 0
    score_w = 3 * GROUP * BLOCK
    return pl.pallas_call(
        functools.partial(_band_attn_kernel, layer=layer),
        out_shape=jax.ShapeDtypeStruct(q.shape, jnp.bfloat16),
        grid=(batch, N_KV_HEADS),
        in_specs=[
            pl.BlockSpec(memory_space=pltpu.SMEM),
            pl.BlockSpec((seq, GROUP_W), lambda b, g: (b, g)),
            pl.BlockSpec((HEAD_DIM, seq), lambda b, g: (g, b)),
            pl.BlockSpec((seq, GROUP_W), lambda b, g: (b, g)),
        ],
        out_specs=pl.BlockSpec((seq, GROUP_W), lambda b, g: (b, g)),
        scratch_shapes=[
            pltpu.VMEM((BLOCK, score_w), jnp.float32), pltpu.VMEM((BLOCK, score_w), jnp.float32),
            pltpu.VMEM((BLOCK, score_w), jnp.bfloat16), pltpu.VMEM((BLOCK, score_w), jnp.bfloat16),
            pltpu.VMEM((BLOCK, GROUP_W), jnp.float32), pltpu.VMEM((BLOCK, GROUP_W), jnp.float32),
        ],
        compiler_params=pltpu.CompilerParams(
            dimension_semantics=("arbitrary", "arbitrary"), vmem_limit_bytes=VMEM_LIMIT),
        name="band_attn",
    )(sink, q, kt, vz)


def _mix_tail_kernel(x_ref, a_ref, cb_ref, u_ref, up_ref, un_ref, sga_ref, sgb_ref,
                     cw_ref, wa_ref, wc_ref, wo_ref, g_ref, o_ref, *, pos_blocks):
    i = pl.program_id(0)
    tm = x_ref.shape[0]
    u = u_ref[...].astype(jnp.float32)
    first = (i % pos_blocks) == 0
    last = (i % pos_blocks) == pos_blocks - 1
    halo_prev = jnp.where(first, 0.0, up_ref[SUBLANES - 1:SUBLANES, :].astype(jnp.float32))
    halo_next = jnp.where(last, 0.0, un_ref[0:1, :].astype(jnp.float32))
    rows = lax.broadcasted_iota(jnp.int32, (tm, 1), 0)
    u_before = jnp.where(rows == 0, halo_prev, pltpu.roll(u, 1, axis=0))
    u_after = jnp.where(rows == tm - 1, halo_next, pltpu.roll(u, tm - 1, axis=0))
    cw = cw_ref[...]
    conv = u_before * cw[0:1, :] + u * cw[1:2, :] + u_after * cw[2:3, :]
    c = (cb_ref[...].astype(jnp.float32) * conv).astype(jnp.bfloat16)
    y_conv = jnp.dot(c, wc_ref[...], preferred_element_type=jnp.float32)
    y_attn = jnp.dot(a_ref[...], wa_ref[...], preferred_element_type=jnp.float32)
    mix = sga_ref[...].astype(jnp.float32) * y_attn + sgb_ref[...].astype(jnp.float32) * y_conv
    y = jnp.dot(mix.astype(jnp.bfloat16), wo_ref[...], preferred_element_type=jnp.float32)
    o_ref[...] = x_ref[...] + _rms_norm_f32(y, g_ref[...])


def _mix_tail(x, attn, cb, u, sga, sgb, conv_w, wa, wc, wo, g, layer, seq):
    n = x.shape[0]
    tm = TOKEN_TILE
    pos_blocks = seq // tm
    halo_blocks = tm // SUBLANES
    row = lambda i: (i, 0)
    prev = lambda i: (jnp.maximum(i * halo_blocks - 1, 0), 0)
    nxt = lambda i: (jnp.minimum((i + 1) * halo_blocks, n // SUBLANES - 1), 0)
    wspec = pl.BlockSpec((None, D_MODEL, D_MODEL), lambda i: (layer, 0, 0))
    tile = pl.BlockSpec((tm, D_MODEL), row)
    return pl.pallas_call(
        functools.partial(_mix_tail_kernel, pos_blocks=pos_blocks),
        out_shape=jax.ShapeDtypeStruct(x.shape, x.dtype),
        grid=(n // tm,),
        in_specs=[
            tile, tile, tile, tile,
            pl.BlockSpec((SUBLANES, D_MODEL), prev), pl.BlockSpec((SUBLANES, D_MODEL), nxt),
            tile, tile,
            pl.BlockSpec((None, CONV_K, D_MODEL), lambda i: (layer, 0, 0)),
            wspec, wspec, wspec,
            pl.BlockSpec((None, 1, D_MODEL), lambda i: (layer, 0, 0)),
        ],
        out_specs=tile,
        compiler_params=pltpu.CompilerParams(
            dimension_semantics=("arbitrary",), vmem_limit_bytes=VMEM_LIMIT),
        name="mix_tail",
    )(x, attn, cb, u, u, u, sga, sgb, conv_w, wa, wc, wo, g)


def _ffn_kernel(x_ref, gpre_ref, wg_ref, wu_ref, wd_ref, gpost_ref, o_ref, act_ref):
    x = x_ref[...]
    h = _rms_norm_f32(x, gpre_ref[...]).astype(jnp.bfloat16)
    for j in range(0, D_FF, COL_CHUNK):
        w = min(COL_CHUNK, D_FF - j)
        gate = jnp.dot(h, wg_ref[:, j:j + w], preferred_element_type=jnp.float32)
        up = jnp.dot(h, wu_ref[:, j:j + w], preferred_element_type=jnp.float32)
        act_ref[:, j:j + w] = (gate * jax.nn.sigmoid(gate) * up).astype(act_ref.dtype)
    f = jnp.dot(act_ref[...], wd_ref[...], preferred_element_type=jnp.float32)
    o_ref[...] = x + _rms_norm_f32(f, gpost_ref[...])


def _ffn(x, gpre, wg, wu, wd, gpost, layer):
    n = x.shape[0]
    tm = TOKEN_TILE
    tile = pl.BlockSpec((tm, D_MODEL), lambda i: (i, 0))
    gspec = pl.BlockSpec((None, 1, D_MODEL), lambda i: (layer, 0, 0))
    return pl.pallas_call(
        _ffn_kernel,
        out_shape=jax.ShapeDtypeStruct(x.shape, x.dtype),
        grid=(n // tm,),
        in_specs=[
            tile, gspec,
            pl.BlockSpec((None, D_MODEL, D_FF), lambda i: (layer, 0, 0)),
            pl.BlockSpec((None, D_MODEL, D_FF), lambda i: (layer, 0, 0)),
            pl.BlockSpec((None, D_FF, D_MODEL), lambda i: (layer, 0, 0)),
            gspec,
        ],
        out_specs=tile,
        scratch_shapes=[pltpu.VMEM((tm, D_FF), jnp.bfloat16)],
        compiler_params=pltpu.CompilerParams(
            dimension_semantics=("arbitrary",), vmem_limit_bytes=VMEM_LIMIT),
        name="ffn",
    )(x, gpre, wg, wu, wd, gpost)


def _rope_lane_tables(seq):
    pos = jnp.arange(seq, dtype=jnp.float32)
    inv = jnp.power(ROPE_THETA, -jnp.arange(0, ROT_DIM, 2, dtype=jnp.float32) / ROT_DIM)
    ang = pos[:, None] * inv[None, :]
    cos, sin = jnp.cos(ang), jnp.sin(ang)
    ones = jnp.ones((seq, HEAD_DIM - ROT_DIM), jnp.float32)
    zeros_r = jnp.zeros((seq, HEAD_DIM - ROT_DIM), jnp.float32)
    zeros_h = jnp.zeros((seq, ROT_HALF), jnp.float32)
    c = jnp.concatenate([cos, cos, ones], axis=1)
    sa = jnp.concatenate([-sin, zeros_h, zeros_r], axis=1)
    sb = jnp.concatenate([zeros_h, sin, zeros_r], axis=1)
    rep = LANES // HEAD_DIM
    return tuple(jnp.tile(t, (1, rep)) for t in (c, sa, sb)), (cos.T, sin.T)


def kernel(x, g_pre_mix, w_in, attn_sink, conv_w, w_attn_proj, w_conv_proj, w_out,
           g_post_mix, g_pre_ffn, w_gate, w_up, w_down, g_post_ffn):
    batch, seq, d = x.shape
    depth = w_in.shape[0]
    assert d == D_MODEL and seq % TOKEN_TILE == 0 and seq % BLOCK == 0
    bf = jnp.bfloat16
    w_kt = jnp.swapaxes(w_in[:, :, _OFF_K:_OFF_K + KV_W], 1, 2).astype(bf)
    w_in, w_attn_proj, w_conv_proj, w_out, w_gate, w_up, w_down = (
        w.astype(bf) for w in (w_in, w_attn_proj, w_conv_proj, w_out, w_gate, w_up, w_down))
    gains = [g.reshape(depth, 1, D_MODEL) for g in (g_pre_mix, g_post_mix, g_pre_ffn, g_post_ffn)]
    tabs, tabs_t = _rope_lane_tables(seq)
    xf = x.reshape(batch * seq, d)
    for layer in range(depth):
        q, kt, vz, cb, u, sga, sgb = _in_proj(xf, gains[0], w_in, w_kt, layer, tabs, tabs_t, seq)
        attn = _band_attn(q, kt, vz, attn_sink, layer, batch, seq)
        xf = _mix_tail(xf, attn, cb, u, sga, sgb, conv_w, w_attn_proj, w_conv_proj, w_out,
                       gains[1], layer, seq)
        xf = _ffn(xf, gains[2], w_gate, w_up, w_down, gains[3], layer)
    return xf.reshape(batch, seq, d)
```

```python
import functools

import jax
import jax.numpy as jnp
from jax import lax
from jax.experimental import pallas as pl
from jax.experimental.pallas import tpu as pltpu

D_MODEL = 1024
N_HEADS = 16
HEAD_DIM = 64
N_KV_HEADS = 4
GROUP = N_HEADS // N_KV_HEADS
ROT_DIM = HEAD_DIM // 4
ROT_HALF = ROT_DIM // 2
ROPE_THETA = 500000.0
WINDOW = 128
BLOCK = 128
CONV_K = 3
D_FF = 2816
Q_W = N_HEADS * HEAD_DIM
KV_W = N_KV_HEADS * HEAD_DIM
GROUP_W = GROUP * HEAD_DIM
IN_COLS = Q_W + 2 * KV_W + 3 * D_MODEL + 2 * D_MODEL
RMS_EPS = 1e-6
NEG_INF = -1e30
LOG2E = 1.4426950408889634

LANES = 128
SUBLANES = 8
TOKEN_TILE = 512
COL_CHUNK = 512
ROW_SUB = 256
SLAB = 16
UNITS_PER_ITER = 2
VMEM_LIMIT = 56 * 1024 * 1024

_OFF_Q = 0
_OFF_K = Q_W
_OFF_V = Q_W + KV_W
_OFF_CB = Q_W + 2 * KV_W
_OFF_CC = _OFF_CB + D_MODEL
_OFF_CX = _OFF_CC + D_MODEL
_OFF_GA = _OFF_CX + D_MODEL
_OFF_GB = _OFF_GA + D_MODEL


def _rms_norm_f32(x, g):
    return x * lax.rsqrt(jnp.mean(x * x, axis=-1, keepdims=True) + RMS_EPS) * g


def _rope_chunk(z, c, sa, sb):
    return z * c + pltpu.roll(z, LANES - ROT_HALF, axis=1) * sa + pltpu.roll(z, ROT_HALF, axis=1) * sb


def _in_proj_kernel(x_ref, g_ref, w_ref, c_ref, sa_ref, sb_ref, ct_ref, st_ref,
                    q_ref, kt_ref, vz_ref, cb_ref, u_ref, sga_ref, sgb_ref):
    scale = HEAD_DIM ** -0.5 * LOG2E
    subs = [slice(r, r + ROW_SUB) for r in range(0, x_ref.shape[0], ROW_SUB)]
    hs = [_rms_norm_f32(x_ref[rs, :], g_ref[...]).astype(jnp.bfloat16) for rs in subs]
    for rs, h in zip(subs, hs):
        def proj(off, width):
            return jnp.dot(h, w_ref[:, off:off + width], preferred_element_type=jnp.float32)

        c, sa, sb = c_ref[rs, :], sa_ref[rs, :], sb_ref[rs, :]

        def rope(z):
            return jnp.concatenate(
                [_rope_chunk(z[:, j:j + LANES], c, sa, sb) for j in range(0, z.shape[1], LANES)], axis=1)

        for j in range(0, Q_W, COL_CHUNK):
            q_ref[rs, j:j + COL_CHUNK] = (rope(proj(_OFF_Q + j, COL_CHUNK)) * scale).astype(q_ref.dtype)

        kt = proj(_OFF_K, KV_W).T
        ct, st = ct_ref[:, rs], st_ref[:, rs]
        pieces = []
        for g in range(N_KV_HEADS):
            base = g * HEAD_DIM
            x1 = kt[base:base + ROT_HALF]
            x2 = kt[base + ROT_HALF:base + ROT_DIM]
            pieces += [x1 * ct - x2 * st, x2 * ct + x1 * st, kt[base + ROT_DIM:base + HEAD_DIM]]
        kt_ref[:, rs] = jnp.concatenate(pieces, axis=0).astype(kt_ref.dtype)

        v = proj(_OFF_V, KV_W)
        zero = jnp.zeros((v.shape[0], HEAD_DIM), jnp.float32)
        halves = []
        for g in range(N_KV_HEADS):
            vg = v[:, g * HEAD_DIM:(g + 1) * HEAD_DIM]
            halves += [vg, zero, zero, vg]
        vz_ref[rs, :] = jnp.concatenate(halves, axis=1).astype(vz_ref.dtype)

        for j in range(0, D_MODEL, COL_CHUNK):
            cols = slice(j, j + COL_CHUNK)
            cb_ref[rs, cols] = proj(_OFF_CB + j, COL_CHUNK).astype(cb_ref.dtype)
            u = proj(_OFF_CC + j, COL_CHUNK) * proj(_OFF_CX + j, COL_CHUNK)
            u_ref[rs, cols] = u.astype(u_ref.dtype)
            sga_ref[rs, cols] = jax.nn.sigmoid(proj(_OFF_GA + j, COL_CHUNK)).astype(sga_ref.dtype)
            sgb_ref[rs, cols] = jax.nn.sigmoid(proj(_OFF_GB + j, COL_CHUNK)).astype(sgb_ref.dtype)


def _in_proj(x, g, w_in, layer, tabs, tabs_t, seq):
    n = x.shape[0]
    tm = TOKEN_TILE
    pos_blocks = seq // tm
    row = lambda i: (i, 0)
    tab = lambda i: (i % pos_blocks, 0)
    tab_t = lambda i: (0, i % pos_blocks)
    bf = jnp.bfloat16
    out_shape = (
        jax.ShapeDtypeStruct((n, Q_W), bf), jax.ShapeDtypeStruct((KV_W, n), bf),
        jax.ShapeDtypeStruct((n, N_KV_HEADS * GROUP_W), bf), jax.ShapeDtypeStruct((n, D_MODEL), bf),
        jax.ShapeDtypeStruct((n, D_MODEL), bf), jax.ShapeDtypeStruct((n, D_MODEL), bf),
        jax.ShapeDtypeStruct((n, D_MODEL), bf))
    return pl.pallas_call(
        _in_proj_kernel,
        out_shape=out_shape,
        grid=(n // tm,),
        in_specs=[
            pl.BlockSpec((tm, D_MODEL), row),
            pl.BlockSpec((None, 1, D_MODEL), lambda i: (layer, 0, 0)),
            pl.BlockSpec((None, D_MODEL, IN_COLS), lambda i: (layer, 0, 0)),
            pl.BlockSpec((tm, LANES), tab), pl.BlockSpec((tm, LANES), tab), pl.BlockSpec((tm, LANES), tab),
            pl.BlockSpec((ROT_HALF, tm), tab_t), pl.BlockSpec((ROT_HALF, tm), tab_t),
        ],
        out_specs=(
            pl.BlockSpec((tm, Q_W), row), pl.BlockSpec((KV_W, tm), lambda i: (0, i)),
            pl.BlockSpec((tm, N_KV_HEADS * GROUP_W), row),
            pl.BlockSpec((tm, D_MODEL), row), pl.BlockSpec((tm, D_MODEL), row),
            pl.BlockSpec((tm, D_MODEL), row), pl.BlockSpec((tm, D_MODEL), row)),
        compiler_params=pltpu.CompilerParams(
            dimension_semantics=("arbitrary",), vmem_limit_bytes=VMEM_LIMIT),
        name="in_proj",
    )(x, g, w_in, *tabs, *tabs_t)


def _band_attn_kernel(sink_ref, q_ref, kt_ref, vz_ref, o_ref,
                      s0_ref, s1_ref, p0_ref, p1_ref, inv0_ref, inv1_ref, *, layer):
    g = pl.program_id(1)
    nb = q_ref.shape[0] // BLOCK
    n_slabs = BLOCK // SLAB
    r = lax.broadcasted_iota(jnp.int32, (SLAB, BLOCK), 0)
    c = lax.broadcasted_iota(jnp.int32, (SLAB, BLOCK), 1)
    left_ok = [c >= r + t * SLAB for t in range(n_slabs)]
    right_ok = [c <= r + t * SLAB for t in range(n_slabs)]
    head_slot = lax.broadcasted_iota(jnp.int32, (SLAB, GROUP_W), 1) // HEAD_DIM
    sinks = [sink_ref[layer, g * GROUP + h] * LOG2E for h in range(GROUP)]
    zk = jnp.zeros((HEAD_DIM, BLOCK), jnp.bfloat16)
    zv = jnp.zeros((BLOCK, LANES), jnp.bfloat16)

    def rows_of(u):
        return pl.ds(pl.multiple_of(u * BLOCK, BLOCK), BLOCK)

    def window(u):
        return (jnp.maximum(u - 1, 0), u, jnp.minimum(u + 1, nb - 1))

    def scores(u, s_ref):
        tiles = []
        for cb in window(u):
            kt = kt_ref[:, rows_of(cb)]
            tiles.append(jnp.concatenate(
                [jnp.concatenate([kt if a == h else zk for a in range(GROUP)], axis=1) for h in range(GROUP)],
                axis=0))
        s_ref[...] = jnp.dot(q_ref[rows_of(u), :], jnp.concatenate(tiles, axis=1),
                             preferred_element_type=jnp.float32)

    def softmax(u, s_ref, p_ref, inv_ref):
        has_left, has_right = u > 0, u < nb - 1
        for t in range(n_slabs):
            rs = slice(t * SLAB, (t + 1) * SLAB)
            lmask = left_ok[t] & has_left
            rmask = right_ok[t] & has_right
            p = [[None] * GROUP for _ in range(3)]
            inv = []
            for h in range(GROUP):
                col = lambda w: slice((w * GROUP + h) * BLOCK, (w * GROUP + h + 1) * BLOCK)
                a0 = jnp.where(lmask, s_ref[rs, col(0)], NEG_INF)
                a1 = s_ref[rs, col(1)]
                a2 = jnp.where(rmask, s_ref[rs, col(2)], NEG_INF)
                m = jnp.maximum(jnp.max(jnp.maximum(jnp.maximum(a0, a1), a2), axis=-1, keepdims=True), sinks[h])
                e0, e1, e2 = jnp.exp2(a0 - m), jnp.exp2(a1 - m), jnp.exp2(a2 - m)
                denom = jnp.sum(e0 + e1 + e2, axis=-1, keepdims=True) + jnp.exp2(sinks[h] - m)
                p[0][h], p[1][h], p[2][h] = e0, e1, e2
                inv.append(1.0 / denom)
            p_ref[rs, :] = jnp.concatenate(
                [p[w][h] for w in range(3) for h in range(GROUP)], axis=1).astype(p_ref.dtype)
            inv_ref[rs, :] = jnp.where(
                head_slot == 0, inv[0], jnp.where(head_slot == 1, inv[1], jnp.where(head_slot == 2, inv[2], inv[3])))

    def values(u, p_ref, inv_ref):
        tiles = []
        for cb in window(u):
            even = vz_ref[rows_of(cb), 0:LANES]
            odd = vz_ref[rows_of(cb), LANES:2 * LANES]
            tiles += [jnp.concatenate([even, zv], axis=1), jnp.concatenate([odd, zv], axis=1),
                      jnp.concatenate([zv, even], axis=1), jnp.concatenate([zv, odd], axis=1)]
        o = jnp.dot(p_ref[...], jnp.concatenate(tiles, axis=0), preferred_element_type=jnp.float32)
        o_ref[rows_of(u), :] = (o * inv_ref[...]).astype(o_ref.dtype)

    s_refs, p_refs, inv_refs = (s0_ref, s1_ref), (p0_ref, p1_ref), (inv0_ref, inv1_ref)
    scores(0, s0_ref)
    p1_ref[...] = jnp.zeros_like(p1_ref)
    inv1_ref[...] = jnp.zeros_like(inv1_ref)

    def body(t, carry):
        for k in range(UNITS_PER_ITER):
            u = t * UNITS_PER_ITER + k
            cur, other = k % 2, (k + 1) % 2
            values(jnp.maximum(u - 1, 0), p_refs[other], inv_refs[other])
            scores(jnp.minimum(u + 1, nb - 1), s_refs[other])
            softmax(u, s_refs[cur], p_refs[cur], inv_refs[cur])
        return carry

    lax.fori_loop(0, nb // UNITS_PER_ITER, body, 0)
    values(nb - 1, p1_ref, inv1_ref)


def _band_attn(q, kt, vz, sink, layer, batch, seq):
    assert UNITS_PER_ITER % 2 == 0 and (seq // BLOCK) % UNITS_PER_ITER == 0
    score_w = 3 * GROUP * BLOCK
    return pl.pallas_call(
        functools.partial(_band_attn_kernel, layer=layer),
        out_shape=jax.ShapeDtypeStruct(q.shape, jnp.bfloat16),
        grid=(batch, N_KV_HEADS),
        in_specs=[
            pl.BlockSpec(memory_space=pltpu.SMEM),
            pl.BlockSpec((seq, GROUP_W), lambda b, g: (b, g)),
            pl.BlockSpec((HEAD_DIM, seq), lambda b, g: (g, b)),
            pl.BlockSpec((seq, GROUP_W), lambda b, g: (b, g)),
        ],
        out_specs=pl.BlockSpec((seq, GROUP_W), lambda b, g: (b, g)),
        scratch_shapes=[
            pltpu.VMEM((BLOCK, score_w), jnp.float32), pltpu.VMEM((BLOCK, score_w), jnp.float32),
            pltpu.VMEM((BLOCK, score_w), jnp.bfloat16), pltpu.VMEM((BLOCK, score_w), jnp.bfloat16),
            pltpu.VMEM((BLOCK, GROUP_W), jnp.float32), pltpu.VMEM((BLOCK, GROUP_W), jnp.float32),
        ],
        compiler_params=pltpu.CompilerParams(
            dimension_semantics=("arbitrary", "arbitrary"), vmem_limit_bytes=VMEM_LIMIT),
        name="band_attn",
    )(sink, q, kt, vz)


def _mix_tail_kernel(x_ref, a_ref, cb_ref, u_ref, up_ref, un_ref, sga_ref, sgb_ref,
                     cw_ref, wa_ref, wc_ref, wo_ref, g_ref, o_ref, *, pos_blocks):
    i = pl.program_id(0)
    tm = x_ref.shape[0]
    u = u_ref[...].astype(jnp.float32)
    first = (i % pos_blocks) == 0
    last = (i % pos_blocks) == pos_blocks - 1
    above = jnp.where(first, 0.0, up_ref[SUBLANES - 1:SUBLANES, :].astype(jnp.float32))
    below = jnp.where(last, 0.0, un_ref[0:1, :].astype(jnp.float32))
    row8 = lax.broadcasted_iota(jnp.int32, (SUBLANES, 1), 0)
    down = pltpu.roll(u, 1, axis=0)
    up = pltpu.roll(u, tm - 1, axis=0)
    u_before = jnp.concatenate([jnp.where(row8 == 0, above, down[:SUBLANES]), down[SUBLANES:]], axis=0)
    u_after = jnp.concatenate(
        [up[:-SUBLANES], jnp.where(row8 == SUBLANES - 1, below, up[-SUBLANES:])], axis=0)
    cw = cw_ref[...]
    conv = u_before * cw[0:1, :] + u * cw[1:2, :] + u_after * cw[2:3, :]
    c = (cb_ref[...].astype(jnp.float32) * conv).astype(jnp.bfloat16)
    y_conv = jnp.dot(c, wc_ref[...], preferred_element_type=jnp.float32)
    y_attn = jnp.dot(a_ref[...], wa_ref[...], preferred_element_type=jnp.float32)
    mix = sga_ref[...].astype(jnp.float32) * y_attn + sgb_ref[...].astype(jnp.float32) * y_conv
    y = jnp.dot(mix.astype(jnp.bfloat16), wo_ref[...], preferred_element_type=jnp.float32)
    o_ref[...] = x_ref[...] + _rms_norm_f32(y, g_ref[...])


def _mix_tail(x, attn, cb, u, sga, sgb, conv_w, wa, wc, wo, g, layer, seq):
    n = x.shape[0]
    tm = TOKEN_TILE
    pos_blocks = seq // tm
    halo_blocks = tm // SUBLANES
    row = lambda i: (i, 0)
    prev = lambda i: (jnp.maximum(i * halo_blocks - 1, 0), 0)
    nxt = lambda i: (jnp.minimum((i + 1) * halo_blocks, n // SUBLANES - 1), 0)
    wspec = pl.BlockSpec((None, D_MODEL, D_MODEL), lambda i: (layer, 0, 0))
    tile = pl.BlockSpec((tm, D_MODEL), row)
    return pl.pallas_call(
        functools.partial(_mix_tail_kernel, pos_blocks=pos_blocks),
        out_shape=jax.ShapeDtypeStruct(x.shape, x.dtype),
        grid=(n // tm,),
        in_specs=[
            tile, tile, tile, tile,
            pl.BlockSpec((SUBLANES, D_MODEL), prev), pl.BlockSpec((SUBLANES, D_MODEL), nxt),
            tile, tile,
            pl.BlockSpec((None, CONV_K, D_MODEL), lambda i: (layer, 0, 0)),
            wspec, wspec, wspec,
            pl.BlockSpec((None, 1, D_MODEL), lambda i: (layer, 0, 0)),
        ],
        out_specs=tile,
        compiler_params=pltpu.CompilerParams(
            dimension_semantics=("arbitrary",), vmem_limit_bytes=VMEM_LIMIT),
        name="mix_tail",
    )(x, attn, cb, u, u, u, sga, sgb, conv_w, wa, wc, wo, g)


def _ffn_kernel(x_ref, gpre_ref, wg_ref, wu_ref, wd_ref, gpost_ref, o_ref, act_ref):
    subs = [slice(r, r + ROW_SUB) for r in range(0, x_ref.shape[0], ROW_SUB)]
    hs = [_rms_norm_f32(x_ref[rs, :], gpre_ref[...]).astype(jnp.bfloat16) for rs in subs]
    for rs, h in zip(subs, hs):
        for j in range(0, D_FF, COL_CHUNK):
            w = min(COL_CHUNK, D_FF - j)
            gate = jnp.dot(h, wg_ref[:, j:j + w], preferred_element_type=jnp.float32)
            up = jnp.dot(h, wu_ref[:, j:j + w], preferred_element_type=jnp.float32)
            act_ref[rs, j:j + w] = (gate * jax.nn.sigmoid(gate) * up).astype(act_ref.dtype)
        f = jnp.dot(act_ref[rs, :], wd_ref[...], preferred_element_type=jnp.float32)
        o_ref[rs, :] = x_ref[rs, :] + _rms_norm_f32(f, gpost_ref[...])


def _ffn(x, gpre, wg, wu, wd, gpost, layer):
    n = x.shape[0]
    tm = TOKEN_TILE
    tile = pl.BlockSpec((tm, D_MODEL), lambda i: (i, 0))
    gspec = pl.BlockSpec((None, 1, D_MODEL), lambda i: (layer, 0, 0))
    return pl.pallas_call(
        _ffn_kernel,
        out_shape=jax.ShapeDtypeStruct(x.shape, x.dtype),
        grid=(n // tm,),
        in_specs=[
            tile, gspec,
            pl.BlockSpec((None, D_MODEL, D_FF), lambda i: (layer, 0, 0)),
            pl.BlockSpec((None, D_MODEL, D_FF), lambda i: (layer, 0, 0)),
            pl.BlockSpec((None, D_FF, D_MODEL), lambda i: (layer, 0, 0)),
            gspec,
        ],
        out_specs=tile,
        scratch_shapes=[pltpu.VMEM((tm, D_FF), jnp.bfloat16)],
        compiler_params=pltpu.CompilerParams(
            dimension_semantics=("arbitrary",), vmem_limit_bytes=VMEM_LIMIT),
        name="ffn",
    )(x, gpre, wg, wu, wd, gpost)


def _rope_lane_tables(seq):
    pos = jnp.arange(seq, dtype=jnp.float32)
    inv = jnp.power(ROPE_THETA, -jnp.arange(0, ROT_DIM, 2, dtype=jnp.float32) / ROT_DIM)
    ang = pos[:, None] * inv[None, :]
    cos, sin = jnp.cos(ang), jnp.sin(ang)
    ones = jnp.ones((seq, HEAD_DIM - ROT_DIM), jnp.float32)
    zeros_r = jnp.zeros((seq, HEAD_DIM - ROT_DIM), jnp.float32)
    zeros_h = jnp.zeros((seq, ROT_HALF), jnp.float32)
    c = jnp.concatenate([cos, cos, ones], axis=1)
    sa = jnp.concatenate([-sin, zeros_h, zeros_r], axis=1)
    sb = jnp.concatenate([zeros_h, sin, zeros_r], axis=1)
    rep = LANES // HEAD_DIM
    return tuple(jnp.tile(t, (1, rep)) for t in (c, sa, sb)), (cos.T, sin.T)


def kernel(x, g_pre_mix, w_in, attn_sink, conv_w, w_attn_proj, w_conv_proj, w_out,
           g_post_mix, g_pre_ffn, w_gate, w_up, w_down, g_post_ffn):
    batch, seq, d = x.shape
    depth = w_in.shape[0]
    assert d == D_MODEL and seq % TOKEN_TILE == 0 and seq % BLOCK == 0
    bf = jnp.bfloat16
    w_in, w_attn_proj, w_conv_proj, w_out, w_gate, w_up, w_down = (
        w.astype(bf) for w in (w_in, w_attn_proj, w_conv_proj, w_out, w_gate, w_up, w_down))
    gains = [g.reshape(depth, 1, D_MODEL) for g in (g_pre_mix, g_post_mix, g_pre_ffn, g_post_ffn)]
    tabs, tabs_t = _rope_lane_tables(seq)
    xf = x.reshape(batch * seq, d)
    for layer in range(depth):
        q, kt, vz, cb, u, sga, sgb = _in_proj(xf, gains[0], w_in, layer, tabs, tabs_t, seq)
        attn = _band_attn(q, kt, vz, attn_sink, layer, batch, seq)
        xf = _mix_tail(xf, attn, cb, u, sga, sgb, conv_w, w_attn_proj, w_conv_proj, w_out,
                       gains[1], layer, seq)
        xf = _ffn(xf, gains[2], w_gate, w_up, w_down, gains[3], layer)
    return xf.reshape(batch, seq, d)
```

```python
import functools

import jax
import jax.numpy as jnp
from jax import lax
from jax.experimental import pallas as pl
from jax.experimental.pallas import tpu as pltpu

D_MODEL = 1024
N_HEADS = 16
HEAD_DIM = 64
N_KV_HEADS = 4
GROUP = N_HEADS // N_KV_HEADS
ROT_DIM = HEAD_DIM // 4
ROT_HALF = ROT_DIM // 2
ROPE_THETA = 500000.0
WINDOW = 128
BLOCK = 128
CONV_K = 3
D_FF = 2816
Q_W = N_HEADS * HEAD_DIM
KV_W = N_KV_HEADS * HEAD_DIM
GROUP_W = GROUP * HEAD_DIM
IN_COLS = Q_W + 2 * KV_W + 3 * D_MODEL + 2 * D_MODEL
RMS_EPS = 1e-6
NEG_INF = -1e30
LOG2E = 1.4426950408889634

LANES = 128
SUBLANES = 8
IN_TILE = 1024
MIX_TILE = 512
FFN_TILE = 1024
COL_CHUNK = 512
ROW_SUB = 256
SLAB = 16
UNITS_PER_ITER = 2
VMEM_LIMIT = 56 * 1024 * 1024
RESIDENT = pl.Buffered(1)

_OFF_Q = 0
_OFF_K = Q_W
_OFF_V = Q_W + KV_W
_OFF_CB = Q_W + 2 * KV_W
_OFF_CC = _OFF_CB + D_MODEL
_OFF_CX = _OFF_CC + D_MODEL
_OFF_GA = _OFF_CX + D_MODEL
_OFF_GB = _OFF_GA + D_MODEL


def _rms_norm_f32(x, g):
    return x * lax.rsqrt(jnp.mean(x * x, axis=-1, keepdims=True) + RMS_EPS) * g


def _rope_chunk(z, c, sa, sb):
    return z * c + pltpu.roll(z, LANES - ROT_HALF, axis=1) * sa + pltpu.roll(z, ROT_HALF, axis=1) * sb


def _in_proj_kernel(x_ref, g_ref, w_ref, c_ref, sa_ref, sb_ref, ct_ref, st_ref,
                    q_ref, kt_ref, vz_ref, cb_ref, u_ref, sga_ref, sgb_ref):
    scale = HEAD_DIM ** -0.5 * LOG2E
    subs = [slice(r, r + ROW_SUB) for r in range(0, x_ref.shape[0], ROW_SUB)]
    hs = [_rms_norm_f32(x_ref[rs, :], g_ref[...]).astype(jnp.bfloat16) for rs in subs]
    for rs, h in zip(subs, hs):
        def proj(off, width):
            return jnp.dot(h, w_ref[:, off:off + width], preferred_element_type=jnp.float32)

        c, sa, sb = c_ref[rs, :], sa_ref[rs, :], sb_ref[rs, :]

        def rope(z):
            return jnp.concatenate(
                [_rope_chunk(z[:, j:j + LANES], c, sa, sb) for j in range(0, z.shape[1], LANES)], axis=1)

        for j in range(0, Q_W, COL_CHUNK):
            q_ref[rs, j:j + COL_CHUNK] = (rope(proj(_OFF_Q + j, COL_CHUNK)) * scale).astype(q_ref.dtype)

        kt = proj(_OFF_K, KV_W).T
        ct, st = ct_ref[:, rs], st_ref[:, rs]
        pieces = []
        for g in range(N_KV_HEADS):
            base = g * HEAD_DIM
            x1 = kt[base:base + ROT_HALF]
            x2 = kt[base + ROT_HALF:base + ROT_DIM]
            pieces += [x1 * ct - x2 * st, x2 * ct + x1 * st, kt[base + ROT_DIM:base + HEAD_DIM]]
        kt_ref[:, rs] = jnp.concatenate(pieces, axis=0).astype(kt_ref.dtype)

        v = proj(_OFF_V, KV_W)
        zero = jnp.zeros((v.shape[0], HEAD_DIM), jnp.float32)
        halves = []
        for g in range(N_KV_HEADS):
            vg = v[:, g * HEAD_DIM:(g + 1) * HEAD_DIM]
            halves += [vg, zero, zero, vg]
        vz_ref[rs, :] = jnp.concatenate(halves, axis=1).astype(vz_ref.dtype)

        for j in range(0, D_MODEL, COL_CHUNK):
            cols = slice(j, j + COL_CHUNK)
            cb_ref[rs, cols] = proj(_OFF_CB + j, COL_CHUNK).astype(cb_ref.dtype)
            u = proj(_OFF_CC + j, COL_CHUNK) * proj(_OFF_CX + j, COL_CHUNK)
            u_ref[rs, cols] = u.astype(u_ref.dtype)
            sga_ref[rs, cols] = jax.nn.sigmoid(proj(_OFF_GA + j, COL_CHUNK)).astype(sga_ref.dtype)
            sgb_ref[rs, cols] = jax.nn.sigmoid(proj(_OFF_GB + j, COL_CHUNK)).astype(sgb_ref.dtype)


def _in_proj(x, g, w_in, layer, tabs, tabs_t, seq):
    n = x.shape[0]
    tm = IN_TILE
    pos_blocks = seq // tm
    row = lambda i: (i, 0)
    tab = lambda i: (i % pos_blocks, 0)
    tab_t = lambda i: (0, i % pos_blocks)
    bf = jnp.bfloat16
    out_shape = (
        jax.ShapeDtypeStruct((n, Q_W), bf), jax.ShapeDtypeStruct((KV_W, n), bf),
        jax.ShapeDtypeStruct((n, N_KV_HEADS * GROUP_W), bf), jax.ShapeDtypeStruct((n, D_MODEL), bf),
        jax.ShapeDtypeStruct((n, D_MODEL), bf), jax.ShapeDtypeStruct((n, D_MODEL), bf),
        jax.ShapeDtypeStruct((n, D_MODEL), bf))
    return pl.pallas_call(
        _in_proj_kernel,
        out_shape=out_shape,
        grid=(n // tm,),
        in_specs=[
            pl.BlockSpec((tm, D_MODEL), row),
            pl.BlockSpec((None, 1, D_MODEL), lambda i: (layer, 0, 0)),
            pl.BlockSpec((None, D_MODEL, IN_COLS), lambda i: (layer, 0, 0), pipeline_mode=RESIDENT),
            pl.BlockSpec((tm, LANES), tab), pl.BlockSpec((tm, LANES), tab), pl.BlockSpec((tm, LANES), tab),
            pl.BlockSpec((ROT_HALF, tm), tab_t), pl.BlockSpec((ROT_HALF, tm), tab_t),
        ],
        out_specs=(
            pl.BlockSpec((tm, Q_W), row), pl.BlockSpec((KV_W, tm), lambda i: (0, i)),
            pl.BlockSpec((tm, N_KV_HEADS * GROUP_W), row),
            pl.BlockSpec((tm, D_MODEL), row), pl.BlockSpec((tm, D_MODEL), row),
            pl.BlockSpec((tm, D_MODEL), row), pl.BlockSpec((tm, D_MODEL), row)),
        compiler_params=pltpu.CompilerParams(
            dimension_semantics=("arbitrary",), vmem_limit_bytes=VMEM_LIMIT),
        name="in_proj",
    )(x, g, w_in, *tabs, *tabs_t)


def _band_attn_kernel(sink_ref, q_ref, kt_ref, vz_ref, o_ref,
                      s0_ref, s1_ref, p0_ref, p1_ref, inv0_ref, inv1_ref, *, layer):
    g = pl.program_id(1)
    nb = q_ref.shape[0] // BLOCK
    n_slabs = BLOCK // SLAB
    r = lax.broadcasted_iota(jnp.int32, (SLAB, BLOCK), 0)
    c = lax.broadcasted_iota(jnp.int32, (SLAB, BLOCK), 1)
    left_ok = [c >= r + t * SLAB for t in range(n_slabs)]
    right_ok = [c <= r + t * SLAB for t in range(n_slabs)]
    head_slot = lax.broadcasted_iota(jnp.int32, (SLAB, GROUP_W), 1) // HEAD_DIM
    sinks = [sink_ref[layer, g * GROUP + h] * LOG2E for h in range(GROUP)]
    zk = jnp.zeros((HEAD_DIM, BLOCK), jnp.bfloat16)
    zv = jnp.zeros((BLOCK, LANES), jnp.bfloat16)

    def rows_of(u):
        return pl.ds(pl.multiple_of(u * BLOCK, BLOCK), BLOCK)

    def window(u):
        return (jnp.maximum(u - 1, 0), u, jnp.minimum(u + 1, nb - 1))

    def scores(u, s_ref):
        tiles = []
        for cb in window(u):
            kt = kt_ref[:, rows_of(cb)]
            tiles.append(jnp.concatenate(
                [jnp.concatenate([kt if a == h else zk for a in range(GROUP)], axis=1) for h in range(GROUP)],
                axis=0))
        s_ref[...] = jnp.dot(q_ref[rows_of(u), :], jnp.concatenate(tiles, axis=1),
                             preferred_element_type=jnp.float32)

    def softmax(u, s_ref, p_ref, inv_ref):
        has_left, has_right = u > 0, u < nb - 1
        for t in range(n_slabs):
            rs = slice(t * SLAB, (t + 1) * SLAB)
            lmask = left_ok[t] & has_left
            rmask = right_ok[t] & has_right
            p = [[None] * GROUP for _ in range(3)]
            inv = []
            for h in range(GROUP):
                col = lambda w: slice((w * GROUP + h) * BLOCK, (w * GROUP + h + 1) * BLOCK)
                a0 = jnp.where(lmask, s_ref[rs, col(0)], NEG_INF)
                a1 = s_ref[rs, col(1)]
                a2 = jnp.where(rmask, s_ref[rs, col(2)], NEG_INF)
                m = jnp.maximum(jnp.max(jnp.maximum(jnp.maximum(a0, a1), a2), axis=-1, keepdims=True), sinks[h])
                e0, e1, e2 = jnp.exp2(a0 - m), jnp.exp2(a1 - m), jnp.exp2(a2 - m)
                denom = jnp.sum(e0 + e1 + e2, axis=-1, keepdims=True) + jnp.exp2(sinks[h] - m)
                p[0][h], p[1][h], p[2][h] = e0, e1, e2
                inv.append(1.0 / denom)
            p_ref[rs, :] = jnp.concatenate(
                [p[w][h] for w in range(3) for h in range(GROUP)], axis=1).astype(p_ref.dtype)
            inv_ref[rs, :] = jnp.where(
                head_slot == 0, inv[0], jnp.where(head_slot == 1, inv[1], jnp.where(head_slot == 2, inv[2], inv[3])))

    def values(u, p_ref, inv_ref):
        tiles = []
        for cb in window(u):
            even = vz_ref[rows_of(cb), 0:LANES]
            odd = vz_ref[rows_of(cb), LANES:2 * LANES]
            tiles += [jnp.concatenate([even, zv], axis=1), jnp.concatenate([odd, zv], axis=1),
                      jnp.concatenate([zv, even], axis=1), jnp.concatenate([zv, odd], axis=1)]
        o = jnp.dot(p_ref[...], jnp.concatenate(tiles, axis=0), preferred_element_type=jnp.float32)
        o_ref[rows_of(u), :] = (o * inv_ref[...]).astype(o_ref.dtype)

    s_refs, p_refs, inv_refs = (s0_ref, s1_ref), (p0_ref, p1_ref), (inv0_ref, inv1_ref)
    scores(0, s0_ref)
    p1_ref[...] = jnp.zeros_like(p1_ref)
    inv1_ref[...] = jnp.zeros_like(inv1_ref)

    def body(t, carry):
        for k in range(UNITS_PER_ITER):
            u = t * UNITS_PER_ITER + k
            cur, other = k % 2, (k + 1) % 2
            values(jnp.maximum(u - 1, 0), p_refs[other], inv_refs[other])
            scores(jnp.minimum(u + 1, nb - 1), s_refs[other])
            softmax(u, s_refs[cur], p_refs[cur], inv_refs[cur])
        return carry

    lax.fori_loop(0, nb // UNITS_PER_ITER, body, 0)
    values(nb - 1, p1_ref, inv1_ref)


def _band_attn(q, kt, vz, sink, layer, batch, seq):
    assert UNITS_PER_ITER % 2 == 0 and (seq // BLOCK) % UNITS_PER_ITER == 0
    score_w = 3 * GROUP * BLOCK
    return pl.pallas_call(
        functools.partial(_band_attn_kernel, layer=layer),
        out_shape=jax.ShapeDtypeStruct(q.shape, jnp.bfloat16),
        grid=(batch, N_KV_HEADS),
        in_specs=[
            pl.BlockSpec(memory_space=pltpu.SMEM),
            pl.BlockSpec((seq, GROUP_W), lambda b, g: (b, g)),
            pl.BlockSpec((HEAD_DIM, seq), lambda b, g: (g, b)),
            pl.BlockSpec((seq, GROUP_W), lambda b, g: (b, g)),
        ],
        out_specs=pl.BlockSpec((seq, GROUP_W), lambda b, g: (b, g)),
        scratch_shapes=[
            pltpu.VMEM((BLOCK, score_w), jnp.float32), pltpu.VMEM((BLOCK, score_w), jnp.float32),
            pltpu.VMEM((BLOCK, score_w), jnp.bfloat16), pltpu.VMEM((BLOCK, score_w), jnp.bfloat16),
            pltpu.VMEM((BLOCK, GROUP_W), jnp.float32), pltpu.VMEM((BLOCK, GROUP_W), jnp.float32),
        ],
        compiler_params=pltpu.CompilerParams(
            dimension_semantics=("arbitrary", "arbitrary"), vmem_limit_bytes=VMEM_LIMIT),
        name="band_attn",
    )(sink, q, kt, vz)


def _mix_tail_kernel(x_ref, a_ref, cb_ref, u_ref, up_ref, un_ref, sga_ref, sgb_ref,
                     cw_ref, wa_ref, wc_ref, wo_ref, g_ref, o_ref, *, pos_blocks):
    i = pl.program_id(0)
    tm = x_ref.shape[0]
    u = u_ref[...].astype(jnp.float32)
    first = (i % pos_blocks) == 0
    last = (i % pos_blocks) == pos_blocks - 1
    above = jnp.where(first, 0.0, up_ref[SUBLANES - 1:SUBLANES, :].astype(jnp.float32))
    below = jnp.where(last, 0.0, un_ref[0:1, :].astype(jnp.float32))
    row8 = lax.broadcasted_iota(jnp.int32, (SUBLANES, 1), 0)
    down = pltpu.roll(u, 1, axis=0)
    up = pltpu.roll(u, tm - 1, axis=0)
    u_before = jnp.concatenate([jnp.where(row8 == 0, above, down[:SUBLANES]), down[SUBLANES:]], axis=0)
    u_after = jnp.concatenate(
        [up[:-SUBLANES], jnp.where(row8 == SUBLANES - 1, below, up[-SUBLANES:])], axis=0)
    cw = cw_ref[...]
    conv = u_before * cw[0:1, :] + u * cw[1:2, :] + u_after * cw[2:3, :]
    c = (cb_ref[...].astype(jnp.float32) * conv).astype(jnp.bfloat16)
    y_conv = jnp.dot(c, wc_ref[...], preferred_element_type=jnp.float32)
    y_attn = jnp.dot(a_ref[...], wa_ref[...], preferred_element_type=jnp.float32)
    mix = sga_ref[...].astype(jnp.float32) * y_attn + sgb_ref[...].astype(jnp.float32) * y_conv
    y = jnp.dot(mix.astype(jnp.bfloat16), wo_ref[...], preferred_element_type=jnp.float32)
    o_ref[...] = x_ref[...] + _rms_norm_f32(y, g_ref[...])


def _mix_tail(x, attn, cb, u, sga, sgb, conv_w, wa, wc, wo, g, layer, seq):
    n = x.shape[0]
    tm = MIX_TILE
    pos_blocks = seq // tm
    halo_blocks = tm // SUBLANES
    row = lambda i: (i, 0)
    prev = lambda i: (jnp.maximum(i * halo_blocks - 1, 0), 0)
    nxt = lambda i: (jnp.minimum((i + 1) * halo_blocks, n // SUBLANES - 1), 0)
    wspec = pl.BlockSpec((None, D_MODEL, D_MODEL), lambda i: (layer, 0, 0), pipeline_mode=RESIDENT)
    tile = pl.BlockSpec((tm, D_MODEL), row)
    return pl.pallas_call(
        functools.partial(_mix_tail_kernel, pos_blocks=pos_blocks),
        out_shape=jax.ShapeDtypeStruct(x.shape, x.dtype),
        grid=(n // tm,),
        in_specs=[
            tile, tile, tile, tile,
            pl.BlockSpec((SUBLANES, D_MODEL), prev), pl.BlockSpec((SUBLANES, D_MODEL), nxt),
            tile, tile,
            pl.BlockSpec((None, CONV_K, D_MODEL), lambda i: (layer, 0, 0)),
            wspec, wspec, wspec,
            pl.BlockSpec((None, 1, D_MODEL), lambda i: (layer, 0, 0)),
        ],
        out_specs=tile,
        compiler_params=pltpu.CompilerParams(
            dimension_semantics=("arbitrary",), vmem_limit_bytes=VMEM_LIMIT),
        name="mix_tail",
    )(x, attn, cb, u, u, u, sga, sgb, conv_w, wa, wc, wo, g)


def _ffn_kernel(x_ref, gpre_ref, wg_ref, wu_ref, wd_ref, gpost_ref, o_ref, act_ref):
    subs = [slice(r, r + ROW_SUB) for r in range(0, x_ref.shape[0], ROW_SUB)]
    hs = [_rms_norm_f32(x_ref[rs, :], gpre_ref[...]).astype(jnp.bfloat16) for rs in subs]
    for rs, h in zip(subs, hs):
        for j in range(0, D_FF, COL_CHUNK):
            w = min(COL_CHUNK, D_FF - j)
            gate = jnp.dot(h, wg_ref[:, j:j + w], preferred_element_type=jnp.float32)
            up = jnp.dot(h, wu_ref[:, j:j + w], preferred_element_type=jnp.float32)
            act_ref[rs, j:j + w] = (gate * jax.nn.sigmoid(gate) * up).astype(act_ref.dtype)
        f = jnp.dot(act_ref[rs, :], wd_ref[...], preferred_element_type=jnp.float32)
        o_ref[rs, :] = x_ref[rs, :] + _rms_norm_f32(f, gpost_ref[...])


def _ffn(x, gpre, wg, wu, wd, gpost, layer):
    n = x.shape[0]
    tm = FFN_TILE
    tile = pl.BlockSpec((tm, D_MODEL), lambda i: (i, 0))
    gspec = pl.BlockSpec((None, 1, D_MODEL), lambda i: (layer, 0, 0))
    return pl.pallas_call(
        _ffn_kernel,
        out_shape=jax.ShapeDtypeStruct(x.shape, x.dtype),
        grid=(n // tm,),
        in_specs=[
            tile, gspec,
            pl.BlockSpec((None, D_MODEL, D_FF), lambda i: (layer, 0, 0), pipeline_mode=RESIDENT),
            pl.BlockSpec((None, D_MODEL, D_FF), lambda i: (layer, 0, 0), pipeline_mode=RESIDENT),
            pl.BlockSpec((None, D_FF, D_MODEL), lambda i: (layer, 0, 0), pipeline_mode=RESIDENT),
            gspec,
        ],
        out_specs=tile,
        scratch_shapes=[pltpu.VMEM((tm, D_FF), jnp.bfloat16)],
        compiler_params=pltpu.CompilerParams(
            dimension_semantics=("arbitrary",), vmem_limit_bytes=VMEM_LIMIT),
        name="ffn",
    )(x, gpre, wg, wu, wd, gpost)


def _rope_lane_tables(seq):
    pos = jnp.arange(seq, dtype=jnp.float32)
    inv = jnp.power(ROPE_THETA, -jnp.arange(0, ROT_DIM, 2, dtype=jnp.float32) / ROT_DIM)
    ang = pos[:, None] * inv[None, :]
    cos, sin = jnp.cos(ang), jnp.sin(ang)
    ones = jnp.ones((seq, HEAD_DIM - ROT_DIM), jnp.float32)
    zeros_r = jnp.zeros((seq, HEAD_DIM - ROT_DIM), jnp.float32)
    zeros_h = jnp.zeros((seq, ROT_HALF), jnp.float32)
    c = jnp.concatenate([cos, cos, ones], axis=1)
    sa = jnp.concatenate([-sin, zeros_h, zeros_r], axis=1)
    sb = jnp.concatenate([zeros_h, sin, zeros_r], axis=1)
    rep = LANES // HEAD_DIM
    return tuple(jnp.tile(t, (1, rep)) for t in (c, sa, sb)), (cos.T, sin.T)


def kernel(x, g_pre_mix, w_in, attn_sink, conv_w, w_attn_proj, w_conv_proj, w_out,
           g_post_mix, g_pre_ffn, w_gate, w_up, w_down, g_post_ffn):
    batch, seq, d = x.shape
    depth = w_in.shape[0]
    assert d == D_MODEL and seq % max(IN_TILE, MIX_TILE, FFN_TILE) == 0 and seq % BLOCK == 0
    bf = jnp.bfloat16
    w_in, w_attn_proj, w_conv_proj, w_out, w_gate, w_up, w_down = (
        w.astype(bf) for w in (w_in, w_attn_proj, w_conv_proj, w_out, w_gate, w_up, w_down))
    gains = [g.reshape(depth, 1, D_MODEL) for g in (g_pre_mix, g_post_mix, g_pre_ffn, g_post_ffn)]
    tabs, tabs_t = _rope_lane_tables(seq)
    xf = x.reshape(batch * seq, d)
    for layer in range(depth):
        q, kt, vz, cb, u, sga, sgb = _in_proj(xf, gains[0], w_in, layer, tabs, tabs_t, seq)
        attn = _band_attn(q, kt, vz, attn_sink, layer, batch, seq)
        xf = _mix_tail(xf, attn, cb, u, sga, sgb, conv_w, w_attn_proj, w_conv_proj, w_out,
                       gains[1], layer, seq)
        xf = _ffn(xf, gains[2], w_gate, w_up, w_down, gains[3], layer)
    return xf.reshape(batch, seq, d)
```

```python
import functools

import jax
import jax.numpy as jnp
from jax import lax
from jax.experimental import pallas as pl
from jax.experimental.pallas import tpu as pltpu

D_MODEL = 1024
N_HEADS = 16
HEAD_DIM = 64
N_KV_HEADS = 4
GROUP = N_HEADS // N_KV_HEADS
ROT_DIM = HEAD_DIM // 4
ROT_HALF = ROT_DIM // 2
ROPE_THETA = 500000.0
WINDOW = 128
BLOCK = 128
CONV_K = 3
D_FF = 2816
Q_W = N_HEADS * HEAD_DIM
KV_W = N_KV_HEADS * HEAD_DIM
GROUP_W = GROUP * HEAD_DIM
IN_COLS = Q_W + 2 * KV_W + 3 * D_MODEL + 2 * D_MODEL
RMS_EPS = 1e-6
NEG_INF = -1e30
LOG2E = 1.4426950408889634

LANES = 128
SUBLANES = 8
IN_TILE = 1024
MIX_TILE = 512
FFN_TILE = 1024
COL_CHUNK = 512
ROW_SUB = 256
SLAB = 16
UNITS_PER_ITER = 2
VMEM_LIMIT = 56 * 1024 * 1024
RESIDENT = pl.Buffered(1)

_OFF_Q = 0
_OFF_K = Q_W
_OFF_V = Q_W + KV_W
_OFF_CB = Q_W + 2 * KV_W
_OFF_CC = _OFF_CB + D_MODEL
_OFF_CX = _OFF_CC + D_MODEL
_OFF_GA = _OFF_CX + D_MODEL
_OFF_GB = _OFF_GA + D_MODEL


def _rms_norm_f32(x, g):
    return x * lax.rsqrt(jnp.mean(x * x, axis=-1, keepdims=True) + RMS_EPS) * g


def _rope_chunk(z, c, sa, sb):
    return z * c + pltpu.roll(z, LANES - ROT_HALF, axis=1) * sa + pltpu.roll(z, ROT_HALF, axis=1) * sb


def _resident(shape):
    return pl.BlockSpec(shape, lambda *_: (0,) * len(shape), pipeline_mode=RESIDENT)


def _in_proj_kernel(x_ref, g_ref, w_ref, c_ref, sa_ref, sb_ref, ct_ref, st_ref,
                    q_ref, kt_ref, vz_ref, cb_ref, u_ref, sga_ref, sgb_ref):
    scale = HEAD_DIM ** -0.5 * LOG2E
    subs = [slice(r, r + ROW_SUB) for r in range(0, x_ref.shape[0], ROW_SUB)]
    hs = [_rms_norm_f32(x_ref[rs, :], g_ref[...]).astype(jnp.bfloat16) for rs in subs]
    for rs, h in zip(subs, hs):
        def proj(off, width):
            return jnp.dot(h, w_ref[:, off:off + width], preferred_element_type=jnp.float32)

        c, sa, sb = c_ref[rs, :], sa_ref[rs, :], sb_ref[rs, :]

        def rope(z):
            return jnp.concatenate(
                [_rope_chunk(z[:, j:j + LANES], c, sa, sb) for j in range(0, z.shape[1], LANES)], axis=1)

        for j in range(0, Q_W, COL_CHUNK):
            q_ref[rs, j:j + COL_CHUNK] = (rope(proj(_OFF_Q + j, COL_CHUNK)) * scale).astype(q_ref.dtype)

        kt = proj(_OFF_K, KV_W).T
        ct, st = ct_ref[:, rs], st_ref[:, rs]
        pieces = []
        for g in range(N_KV_HEADS):
            base = g * HEAD_DIM
            x1 = kt[base:base + ROT_HALF]
            x2 = kt[base + ROT_HALF:base + ROT_DIM]
            pieces += [x1 * ct - x2 * st, x2 * ct + x1 * st, kt[base + ROT_DIM:base + HEAD_DIM]]
        kt_ref[:, rs] = jnp.concatenate(pieces, axis=0).astype(kt_ref.dtype)

        v = proj(_OFF_V, KV_W)
        zero = jnp.zeros((v.shape[0], HEAD_DIM), jnp.float32)
        halves = []
        for g in range(N_KV_HEADS):
            vg = v[:, g * HEAD_DIM:(g + 1) * HEAD_DIM]
            halves += [vg, zero, zero, vg]
        vz_ref[rs, :] = jnp.concatenate(halves, axis=1).astype(vz_ref.dtype)

        for j in range(0, D_MODEL, COL_CHUNK):
            cols = slice(j, j + COL_CHUNK)
            cb_ref[rs, cols] = proj(_OFF_CB + j, COL_CHUNK).astype(cb_ref.dtype)
            u = proj(_OFF_CC + j, COL_CHUNK) * proj(_OFF_CX + j, COL_CHUNK)
            u_ref[rs, cols] = u.astype(u_ref.dtype)
            sga_ref[rs, cols] = jax.nn.sigmoid(proj(_OFF_GA + j, COL_CHUNK)).astype(sga_ref.dtype)
            sgb_ref[rs, cols] = jax.nn.sigmoid(proj(_OFF_GB + j, COL_CHUNK)).astype(sgb_ref.dtype)


def _in_proj(x, g, w_in, layer, tabs, tabs_t, seq):
    n = x.shape[0]
    tm = IN_TILE
    pos_blocks = seq // tm
    row = lambda i: (i, 0)
    tab = lambda i: (i % pos_blocks, 0)
    tab_t = lambda i: (0, i % pos_blocks)
    bf = jnp.bfloat16
    out_shape = (
        jax.ShapeDtypeStruct((n, Q_W), bf), jax.ShapeDtypeStruct((KV_W, n), bf),
        jax.ShapeDtypeStruct((n, N_KV_HEADS * GROUP_W), bf), jax.ShapeDtypeStruct((n, D_MODEL), bf),
        jax.ShapeDtypeStruct((n, D_MODEL), bf), jax.ShapeDtypeStruct((n, D_MODEL), bf),
        jax.ShapeDtypeStruct((n, D_MODEL), bf))
    return pl.pallas_call(
        _in_proj_kernel,
        out_shape=out_shape,
        grid=(n // tm,),
        in_specs=[
            pl.BlockSpec((tm, D_MODEL), row),
            pl.BlockSpec((None, 1, D_MODEL), lambda i: (layer, 0, 0)),
            _resident((D_MODEL, IN_COLS)),
            pl.BlockSpec((tm, LANES), tab), pl.BlockSpec((tm, LANES), tab), pl.BlockSpec((tm, LANES), tab),
            pl.BlockSpec((ROT_HALF, tm), tab_t), pl.BlockSpec((ROT_HALF, tm), tab_t),
        ],
        out_specs=(
            pl.BlockSpec((tm, Q_W), row), pl.BlockSpec((KV_W, tm), lambda i: (0, i)),
            pl.BlockSpec((tm, N_KV_HEADS * GROUP_W), row),
            pl.BlockSpec((tm, D_MODEL), row), pl.BlockSpec((tm, D_MODEL), row),
            pl.BlockSpec((tm, D_MODEL), row), pl.BlockSpec((tm, D_MODEL), row)),
        compiler_params=pltpu.CompilerParams(
            dimension_semantics=("arbitrary",), vmem_limit_bytes=VMEM_LIMIT),
        name="in_proj",
    )(x, g, w_in, *tabs, *tabs_t)


def _band_attn_kernel(sink_ref, q_ref, kt_ref, vz_ref, *rest, layer, n_cast):
    cast_src = rest[:n_cast]
    o_ref = rest[n_cast]
    cast_dst = rest[n_cast + 1:2 * n_cast + 1]
    s0_ref, s1_ref, p0_ref, p1_ref, inv0_ref, inv1_ref = rest[2 * n_cast + 1:]
    g = pl.program_id(1)

    @pl.when(g == 0)
    def _():
        for src, dst in zip(cast_src, cast_dst):
            dst[...] = src[...].astype(dst.dtype)

    nb = q_ref.shape[0] // BLOCK
    n_slabs = BLOCK // SLAB
    r = lax.broadcasted_iota(jnp.int32, (SLAB, BLOCK), 0)
    c = lax.broadcasted_iota(jnp.int32, (SLAB, BLOCK), 1)
    left_ok = [c >= r + t * SLAB for t in range(n_slabs)]
    right_ok = [c <= r + t * SLAB for t in range(n_slabs)]
    head_slot = lax.broadcasted_iota(jnp.int32, (SLAB, GROUP_W), 1) // HEAD_DIM
    sinks = [sink_ref[layer, g * GROUP + h] * LOG2E for h in range(GROUP)]
    zk = jnp.zeros((HEAD_DIM, BLOCK), jnp.bfloat16)
    zv = jnp.zeros((BLOCK, LANES), jnp.bfloat16)

    def rows_of(u):
        return pl.ds(pl.multiple_of(u * BLOCK, BLOCK), BLOCK)

    def window(u):
        return (jnp.maximum(u - 1, 0), u, jnp.minimum(u + 1, nb - 1))

    def scores(u, s_ref):
        tiles = []
        for cb in window(u):
            kt = kt_ref[:, rows_of(cb)]
            tiles.append(jnp.concatenate(
                [jnp.concatenate([kt if a == h else zk for a in range(GROUP)], axis=1) for h in range(GROUP)],
                axis=0))
        s_ref[...] = jnp.dot(q_ref[rows_of(u), :], jnp.concatenate(tiles, axis=1),
                             preferred_element_type=jnp.float32)

    def softmax(u, s_ref, p_ref, inv_ref):
        has_left, has_right = u > 0, u < nb - 1
        for t in range(n_slabs):
            rs = slice(t * SLAB, (t + 1) * SLAB)
            lmask = left_ok[t] & has_left
            rmask = right_ok[t] & has_right
            p = [[None] * GROUP for _ in range(3)]
            inv = []
            for h in range(GROUP):
                col = lambda w: slice((w * GROUP + h) * BLOCK, (w * GROUP + h + 1) * BLOCK)
                a0 = jnp.where(lmask, s_ref[rs, col(0)], NEG_INF)
                a1 = s_ref[rs, col(1)]
                a2 = jnp.where(rmask, s_ref[rs, col(2)], NEG_INF)
                m = jnp.maximum(jnp.max(jnp.maximum(jnp.maximum(a0, a1), a2), axis=-1, keepdims=True), sinks[h])
                e0, e1, e2 = jnp.exp2(a0 - m), jnp.exp2(a1 - m), jnp.exp2(a2 - m)
                denom = jnp.sum(e0 + e1 + e2, axis=-1, keepdims=True) + jnp.exp2(sinks[h] - m)
                p[0][h], p[1][h], p[2][h] = e0, e1, e2
                inv.append(1.0 / denom)
            p_ref[rs, :] = jnp.concatenate(
                [p[w][h] for w in range(3) for h in range(GROUP)], axis=1).astype(p_ref.dtype)
            inv_ref[rs, :] = jnp.where(
                head_slot == 0, inv[0], jnp.where(head_slot == 1, inv[1], jnp.where(head_slot == 2, inv[2], inv[3])))

    def values(u, p_ref, inv_ref):
        tiles = []
        for cb in window(u):
            even = vz_ref[rows_of(cb), 0:LANES]
            odd = vz_ref[rows_of(cb), LANES:2 * LANES]
            tiles += [jnp.concatenate([even, zv], axis=1), jnp.concatenate([odd, zv], axis=1),
                      jnp.concatenate([zv, even], axis=1), jnp.concatenate([zv, odd], axis=1)]
        o = jnp.dot(p_ref[...], jnp.concatenate(tiles, axis=0), preferred_element_type=jnp.float32)
        o_ref[rows_of(u), :] = (o * inv_ref[...]).astype(o_ref.dtype)

    s_refs, p_refs, inv_refs = (s0_ref, s1_ref), (p0_ref, p1_ref), (inv0_ref, inv1_ref)
    scores(0, s0_ref)
    p1_ref[...] = jnp.zeros_like(p1_ref)
    inv1_ref[...] = jnp.zeros_like(inv1_ref)

    def body(t, carry):
        for k in range(UNITS_PER_ITER):
            u = t * UNITS_PER_ITER + k
            cur, other = k % 2, (k + 1) % 2
            values(jnp.maximum(u - 1, 0), p_refs[other], inv_refs[other])
            scores(jnp.minimum(u + 1, nb - 1), s_refs[other])
            softmax(u, s_refs[cur], p_refs[cur], inv_refs[cur])
        return carry

    lax.fori_loop(0, nb // UNITS_PER_ITER, body, 0)
    values(nb - 1, p1_ref, inv1_ref)


def _band_attn(q, kt, vz, sink, layer, batch, seq, casts):
    assert UNITS_PER_ITER % 2 == 0 and (seq // BLOCK) % UNITS_PER_ITER == 0
    score_w = 3 * GROUP * BLOCK
    bf = jnp.bfloat16
    cast_in, cast_out, cast_shapes = [], [], []
    for w, lyr in casts:
        rows, cols = w.shape[1] // batch, w.shape[2]
        assert w.shape[1] % batch == 0 and rows % (2 * SUBLANES) == 0
        cast_in.append(pl.BlockSpec((None, rows, cols), functools.partial(lambda b, g, l: (l, b, 0), l=lyr)))
        cast_out.append(pl.BlockSpec((rows, cols), lambda b, g: (b, 0)))
        cast_shapes.append(jax.ShapeDtypeStruct(w.shape[1:], bf))
    tile = pl.BlockSpec((seq, GROUP_W), lambda b, g: (b, g))
    outs = pl.pallas_call(
        functools.partial(_band_attn_kernel, layer=layer, n_cast=len(casts)),
        out_shape=[jax.ShapeDtypeStruct(q.shape, bf)] + cast_shapes,
        grid=(batch, N_KV_HEADS),
        in_specs=[
            pl.BlockSpec(memory_space=pltpu.SMEM),
            tile,
            pl.BlockSpec((HEAD_DIM, seq), lambda b, g: (g, b)),
            tile,
        ] + cast_in,
        out_specs=[tile] + cast_out,
        scratch_shapes=[
            pltpu.VMEM((BLOCK, score_w), jnp.float32), pltpu.VMEM((BLOCK, score_w), jnp.float32),
            pltpu.VMEM((BLOCK, score_w), jnp.bfloat16), pltpu.VMEM((BLOCK, score_w), jnp.bfloat16),
            pltpu.VMEM((BLOCK, GROUP_W), jnp.float32), pltpu.VMEM((BLOCK, GROUP_W), jnp.float32),
        ],
        compiler_params=pltpu.CompilerParams(
            dimension_semantics=("arbitrary", "arbitrary"), vmem_limit_bytes=VMEM_LIMIT),
        name="band_attn",
    )(sink, q, kt, vz, *[w for w, _ in casts])
    return outs[0], outs[1:]


def _mix_tail_kernel(x_ref, a_ref, cb_ref, u_ref, up_ref, un_ref, sga_ref, sgb_ref,
                     cw_ref, wa_ref, wc_ref, wo_ref, g_ref, o_ref, *, pos_blocks):
    i = pl.program_id(0)
    tm = x_ref.shape[0]
    u = u_ref[...].astype(jnp.float32)
    first = (i % pos_blocks) == 0
    last = (i % pos_blocks) == pos_blocks - 1
    above = jnp.where(first, 0.0, up_ref[SUBLANES - 1:SUBLANES, :].astype(jnp.float32))
    below = jnp.where(last, 0.0, un_ref[0:1, :].astype(jnp.float32))
    row8 = lax.broadcasted_iota(jnp.int32, (SUBLANES, 1), 0)
    down = pltpu.roll(u, 1, axis=0)
    up = pltpu.roll(u, tm - 1, axis=0)
    u_before = jnp.concatenate([jnp.where(row8 == 0, above, down[:SUBLANES]), down[SUBLANES:]], axis=0)
    u_after = jnp.concatenate(
        [up[:-SUBLANES], jnp.where(row8 == SUBLANES - 1, below, up[-SUBLANES:])], axis=0)
    cw = cw_ref[...]
    conv = u_before * cw[0:1, :] + u * cw[1:2, :] + u_after * cw[2:3, :]
    c = (cb_ref[...].astype(jnp.float32) * conv).astype(jnp.bfloat16)
    y_conv = jnp.dot(c, wc_ref[...], preferred_element_type=jnp.float32)
    y_attn = jnp.dot(a_ref[...], wa_ref[...], preferred_element_type=jnp.float32)
    mix = sga_ref[...].astype(jnp.float32) * y_attn + sgb_ref[...].astype(jnp.float32) * y_conv
    y = jnp.dot(mix.astype(jnp.bfloat16), wo_ref[...], preferred_element_type=jnp.float32)
    o_ref[...] = x_ref[...] + _rms_norm_f32(y, g_ref[...])


def _mix_tail(x, attn, cb, u, sga, sgb, conv_w, wa, wc, wo, g, layer, seq):
    n = x.shape[0]
    tm = MIX_TILE
    pos_blocks = seq // tm
    halo_blocks = tm // SUBLANES
    row = lambda i: (i, 0)
    prev = lambda i: (jnp.maximum(i * halo_blocks - 1, 0), 0)
    nxt = lambda i: (jnp.minimum((i + 1) * halo_blocks, n // SUBLANES - 1), 0)
    wspec = _resident((D_MODEL, D_MODEL))
    tile = pl.BlockSpec((tm, D_MODEL), row)
    return pl.pallas_call(
        functools.partial(_mix_tail_kernel, pos_blocks=pos_blocks),
        out_shape=jax.ShapeDtypeStruct(x.shape, x.dtype),
        grid=(n // tm,),
        in_specs=[
            tile, tile, tile, tile,
            pl.BlockSpec((SUBLANES, D_MODEL), prev), pl.BlockSpec((SUBLANES, D_MODEL), nxt),
            tile, tile,
            pl.BlockSpec((None, CONV_K, D_MODEL), lambda i: (layer, 0, 0)),
            wspec, wspec, wspec,
            pl.BlockSpec((None, 1, D_MODEL), lambda i: (layer, 0, 0)),
        ],
        out_specs=tile,
        compiler_params=pltpu.CompilerParams(
            dimension_semantics=("arbitrary",), vmem_limit_bytes=VMEM_LIMIT),
        name="mix_tail",
    )(x, attn, cb, u, u, u, sga, sgb, conv_w, wa, wc, wo, g)


def _ffn_kernel(x_ref, gpre_ref, wg_ref, wu_ref, wd_ref, gpost_ref, o_ref, act_ref):
    subs = [slice(r, r + ROW_SUB) for r in range(0, x_ref.shape[0], ROW_SUB)]
    hs = [_rms_norm_f32(x_ref[rs, :], gpre_ref[...]).astype(jnp.bfloat16) for rs in subs]
    for rs, h in zip(subs, hs):
        for j in range(0, D_FF, COL_CHUNK):
            w = min(COL_CHUNK, D_FF - j)
            gate = jnp.dot(h, wg_ref[:, j:j + w], preferred_element_type=jnp.float32)
            up = jnp.dot(h, wu_ref[:, j:j + w], preferred_element_type=jnp.float32)
            act_ref[rs, j:j + w] = (gate * jax.nn.sigmoid(gate) * up).astype(act_ref.dtype)
        f = jnp.dot(act_ref[rs, :], wd_ref[...], preferred_element_type=jnp.float32)
        o_ref[rs, :] = x_ref[rs, :] + _rms_norm_f32(f, gpost_ref[...])


def _ffn(x, gpre, wg, wu, wd, gpost, layer):
    n = x.shape[0]
    tm = FFN_TILE
    tile = pl.BlockSpec((tm, D_MODEL), lambda i: (i, 0))
    gspec = pl.BlockSpec((None, 1, D_MODEL), lambda i: (layer, 0, 0))
    return pl.pallas_call(
        _ffn_kernel,
        out_shape=jax.ShapeDtypeStruct(x.shape, x.dtype),
        grid=(n // tm,),
        in_specs=[
            tile, gspec,
            _resident((D_MODEL, D_FF)), _resident((D_MODEL, D_FF)), _resident((D_FF, D_MODEL)),
            gspec,
        ],
        out_specs=tile,
        scratch_shapes=[pltpu.VMEM((tm, D_FF), jnp.bfloat16)],
        compiler_params=pltpu.CompilerParams(
            dimension_semantics=("arbitrary",), vmem_limit_bytes=VMEM_LIMIT),
        name="ffn",
    )(x, gpre, wg, wu, wd, gpost)


def _rope_lane_tables(seq):
    pos = jnp.arange(seq, dtype=jnp.float32)
    inv = jnp.power(ROPE_THETA, -jnp.arange(0, ROT_DIM, 2, dtype=jnp.float32) / ROT_DIM)
    ang = pos[:, None] * inv[None, :]
    cos, sin = jnp.cos(ang), jnp.sin(ang)
    ones = jnp.ones((seq, HEAD_DIM - ROT_DIM), jnp.float32)
    zeros_r = jnp.zeros((seq, HEAD_DIM - ROT_DIM), jnp.float32)
    zeros_h = jnp.zeros((seq, ROT_HALF), jnp.float32)
    c = jnp.concatenate([cos, cos, ones], axis=1)
    sa = jnp.concatenate([-sin, zeros_h, zeros_r], axis=1)
    sb = jnp.concatenate([zeros_h, sin, zeros_r], axis=1)
    rep = LANES // HEAD_DIM
    return tuple(jnp.tile(t, (1, rep)) for t in (c, sa, sb)), (cos.T, sin.T)


def kernel(x, g_pre_mix, w_in, attn_sink, conv_w, w_attn_proj, w_conv_proj, w_out,
           g_post_mix, g_pre_ffn, w_gate, w_up, w_down, g_post_ffn):
    batch, seq, d = x.shape
    depth = w_in.shape[0]
    assert d == D_MODEL and seq % max(IN_TILE, MIX_TILE, FFN_TILE) == 0 and seq % BLOCK == 0
    gains = [g.reshape(depth, 1, D_MODEL) for g in (g_pre_mix, g_post_mix, g_pre_ffn, g_post_ffn)]
    tabs, tabs_t = _rope_lane_tables(seq)
    xf = x.reshape(batch * seq, d)
    w_in_bf = w_in[0].astype(jnp.bfloat16)
    for layer in range(depth):
        q, kt, vz, cb, u, sga, sgb = _in_proj(xf, gains[0], w_in_bf, layer, tabs, tabs_t, seq)
        casts = [(w, layer) for w in (w_attn_proj, w_conv_proj, w_out, w_gate, w_up, w_down)]
        if layer + 1 < depth:
            casts.append((w_in, layer + 1))
        attn, cast = _band_attn(q, kt, vz, attn_sink, layer, batch, seq, casts)
        wa, wc, wo, wg, wu, wd = cast[:6]
        if layer + 1 < depth:
            w_in_bf = cast[6]
        xf = _mix_tail(xf, attn, cb, u, sga, sgb, conv_w, wa, wc, wo, gains[1], layer, seq)
        xf = _ffn(xf, gains[2], wg, wu, wd, gains[3], layer)
    return xf.reshape(batch, seq, d)
```

```python
import functools

import jax
import jax.numpy as jnp
from jax import lax
from jax.experimental import pallas as pl
from jax.experimental.pallas import tpu as pltpu

D_MODEL = 1024
N_HEADS = 16
HEAD_DIM = 64
N_KV_HEADS = 4
GROUP = N_HEADS // N_KV_HEADS
ROT_DIM = HEAD_DIM // 4
ROT_HALF = ROT_DIM // 2
ROPE_THETA = 500000.0
WINDOW = 128
BLOCK = 128
CONV_K = 3
D_FF = 2816
Q_W = N_HEADS * HEAD_DIM
KV_W = N_KV_HEADS * HEAD_DIM
GROUP_W = GROUP * HEAD_DIM
IN_COLS = Q_W + 2 * KV_W + 3 * D_MODEL + 2 * D_MODEL
RMS_EPS = 1e-6
NEG_INF = -1e30
LOG2E = 1.4426950408889634

LANES = 128
SUBLANES = 8
IN_TILE = 1024
MIX_TILE = 1024
FFN_TILE = 1024
COL_CHUNK = 512
ROW_SUB = 256
SLAB = 16
UNITS_PER_ITER = 2
VMEM_LIMIT = 56 * 1024 * 1024
RESIDENT = pl.Buffered(1)

_OFF_Q = 0
_OFF_K = Q_W
_OFF_V = Q_W + KV_W
_OFF_CB = Q_W + 2 * KV_W
_OFF_CC = _OFF_CB + D_MODEL
_OFF_CX = _OFF_CC + D_MODEL
_OFF_GA = _OFF_CX + D_MODEL
_OFF_GB = _OFF_GA + D_MODEL


def _rms_norm_f32(x, g):
    return x * lax.rsqrt(jnp.mean(x * x, axis=-1, keepdims=True) + RMS_EPS) * g


def _rope_chunk(z, c, sa, sb):
    return z * c + pltpu.roll(z, LANES - ROT_HALF, axis=1) * sa + pltpu.roll(z, ROT_HALF, axis=1) * sb


def _resident(shape):
    return pl.BlockSpec(shape, lambda *_: (0,) * len(shape), pipeline_mode=RESIDENT)


def _in_proj_kernel(x_ref, g_ref, w_ref, c_ref, sa_ref, sb_ref, ct_ref, st_ref,
                    q_ref, kt_ref, vz_ref, cb_ref, u_ref, sga_ref, sgb_ref):
    scale = HEAD_DIM ** -0.5 * LOG2E
    subs = [slice(r, r + ROW_SUB) for r in range(0, x_ref.shape[0], ROW_SUB)]
    hs = [_rms_norm_f32(x_ref[rs, :], g_ref[...]).astype(jnp.bfloat16) for rs in subs]
    for rs, h in zip(subs, hs):
        def proj(off, width):
            return jnp.dot(h, w_ref[:, off:off + width], preferred_element_type=jnp.float32)

        c, sa, sb = c_ref[rs, :], sa_ref[rs, :], sb_ref[rs, :]

        def rope(z):
            return jnp.concatenate(
                [_rope_chunk(z[:, j:j + LANES], c, sa, sb) for j in range(0, z.shape[1], LANES)], axis=1)

        for j in range(0, Q_W, COL_CHUNK):
            q_ref[rs, j:j + COL_CHUNK] = (rope(proj(_OFF_Q + j, COL_CHUNK)) * scale).astype(q_ref.dtype)

        kt = proj(_OFF_K, KV_W).T
        ct, st = ct_ref[:, rs], st_ref[:, rs]
        pieces = []
        for g in range(N_KV_HEADS):
            base = g * HEAD_DIM
            x1 = kt[base:base + ROT_HALF]
            x2 = kt[base + ROT_HALF:base + ROT_DIM]
            pieces += [x1 * ct - x2 * st, x2 * ct + x1 * st, kt[base + ROT_DIM:base + HEAD_DIM]]
        kt_ref[:, rs] = jnp.concatenate(pieces, axis=0).astype(kt_ref.dtype)

        v = proj(_OFF_V, KV_W)
        zero = jnp.zeros((v.shape[0], HEAD_DIM), jnp.float32)
        halves = []
        for g in range(N_KV_HEADS):
            vg = v[:, g * HEAD_DIM:(g + 1) * HEAD_DIM]
            halves += [vg, zero, zero, vg]
        vz_ref[rs, :] = jnp.concatenate(halves, axis=1).astype(vz_ref.dtype)

        for j in range(0, D_MODEL, COL_CHUNK):
            cols = slice(j, j + COL_CHUNK)
            cb_ref[rs, cols] = proj(_OFF_CB + j, COL_CHUNK).astype(cb_ref.dtype)
            u = proj(_OFF_CC + j, COL_CHUNK) * proj(_OFF_CX + j, COL_CHUNK)
            u_ref[rs, cols] = u.astype(u_ref.dtype)
            sga_ref[rs, cols] = jax.nn.sigmoid(proj(_OFF_GA + j, COL_CHUNK)).astype(sga_ref.dtype)
            sgb_ref[rs, cols] = jax.nn.sigmoid(proj(_OFF_GB + j, COL_CHUNK)).astype(sgb_ref.dtype)


def _in_proj(x, g, w_in, layer, tabs, tabs_t, seq):
    n = x.shape[0]
    tm = IN_TILE
    pos_blocks = seq // tm
    row = lambda i: (i, 0)
    tab = lambda i: (i % pos_blocks, 0)
    tab_t = lambda i: (0, i % pos_blocks)
    bf = jnp.bfloat16
    out_shape = (
        jax.ShapeDtypeStruct((n, Q_W), bf), jax.ShapeDtypeStruct((KV_W, n), bf),
        jax.ShapeDtypeStruct((n, N_KV_HEADS * GROUP_W), bf), jax.ShapeDtypeStruct((n, D_MODEL), bf),
        jax.ShapeDtypeStruct((n, D_MODEL), bf), jax.ShapeDtypeStruct((n, D_MODEL), bf),
        jax.ShapeDtypeStruct((n, D_MODEL), bf))
    return pl.pallas_call(
        _in_proj_kernel,
        out_shape=out_shape,
        grid=(n // tm,),
        in_specs=[
            pl.BlockSpec((tm, D_MODEL), row),
            pl.BlockSpec((None, 1, D_MODEL), lambda i: (layer, 0, 0)),
            _resident((D_MODEL, IN_COLS)),
            pl.BlockSpec((tm, LANES), tab), pl.BlockSpec((tm, LANES), tab), pl.BlockSpec((tm, LANES), tab),
            pl.BlockSpec((ROT_HALF, tm), tab_t), pl.BlockSpec((ROT_HALF, tm), tab_t),
        ],
        out_specs=(
            pl.BlockSpec((tm, Q_W), row), pl.BlockSpec((KV_W, tm), lambda i: (0, i)),
            pl.BlockSpec((tm, N_KV_HEADS * GROUP_W), row),
            pl.BlockSpec((tm, D_MODEL), row), pl.BlockSpec((tm, D_MODEL), row),
            pl.BlockSpec((tm, D_MODEL), row), pl.BlockSpec((tm, D_MODEL), row)),
        compiler_params=pltpu.CompilerParams(
            dimension_semantics=("arbitrary",), vmem_limit_bytes=VMEM_LIMIT),
        name="in_proj",
    )(x, g, w_in, *tabs, *tabs_t)


def _band_attn_kernel(sink_ref, q_ref, kt_ref, vz_ref, *rest, layer, n_cast):
    cast_src = rest[:n_cast]
    o_ref = rest[n_cast]
    cast_dst = rest[n_cast + 1:2 * n_cast + 1]
    s0_ref, s1_ref, p0_ref, p1_ref, inv0_ref, inv1_ref = rest[2 * n_cast + 1:]
    g = pl.program_id(1)

    @pl.when(g == 0)
    def _():
        for src, dst in zip(cast_src, cast_dst):
            dst[...] = src[...].astype(dst.dtype)

    nb = q_ref.shape[0] // BLOCK
    n_slabs = BLOCK // SLAB
    r = lax.broadcasted_iota(jnp.int32, (SLAB, BLOCK), 0)
    c = lax.broadcasted_iota(jnp.int32, (SLAB, BLOCK), 1)
    left_ok = [c >= r + t * SLAB for t in range(n_slabs)]
    right_ok = [c <= r + t * SLAB for t in range(n_slabs)]
    head_slot = lax.broadcasted_iota(jnp.int32, (SLAB, GROUP_W), 1) // HEAD_DIM
    sinks = [sink_ref[layer, g * GROUP + h] * LOG2E for h in range(GROUP)]
    zk = jnp.zeros((HEAD_DIM, BLOCK), jnp.bfloat16)
    zv = jnp.zeros((BLOCK, LANES), jnp.bfloat16)

    def rows_of(u):
        return pl.ds(pl.multiple_of(u * BLOCK, BLOCK), BLOCK)

    def window(u):
        return (jnp.maximum(u - 1, 0), u, jnp.minimum(u + 1, nb - 1))

    def scores(u, s_ref):
        tiles = []
        for cb in window(u):
            kt = kt_ref[:, rows_of(cb)]
            tiles.append(jnp.concatenate(
                [jnp.concatenate([kt if a == h else zk for a in range(GROUP)], axis=1) for h in range(GROUP)],
                axis=0))
        s_ref[...] = jnp.dot(q_ref[rows_of(u), :], jnp.concatenate(tiles, axis=1),
                             preferred_element_type=jnp.float32)

    def softmax(u, s_ref, p_ref, inv_ref):
        has_left, has_right = u > 0, u < nb - 1
        for t in range(n_slabs):
            rs = slice(t * SLAB, (t + 1) * SLAB)
            lmask = left_ok[t] & has_left
            rmask = right_ok[t] & has_right
            p = [[None] * GROUP for _ in range(3)]
            inv = []
            for h in range(GROUP):
                col = lambda w: slice((w * GROUP + h) * BLOCK, (w * GROUP + h + 1) * BLOCK)
                a0 = jnp.where(lmask, s_ref[rs, col(0)], NEG_INF)
                a1 = s_ref[rs, col(1)]
                a2 = jnp.where(rmask, s_ref[rs, col(2)], NEG_INF)
                m = jnp.maximum(jnp.max(jnp.maximum(jnp.maximum(a0, a1), a2), axis=-1, keepdims=True), sinks[h])
                e0, e1, e2 = jnp.exp2(a0 - m), jnp.exp2(a1 - m), jnp.exp2(a2 - m)
                denom = jnp.sum(e0 + e1 + e2, axis=-1, keepdims=True) + jnp.exp2(sinks[h] - m)
                p[0][h], p[1][h], p[2][h] = e0, e1, e2
                inv.append(1.0 / denom)
            p_ref[rs, :] = jnp.concatenate(
                [p[w][h] for w in range(3) for h in range(GROUP)], axis=1).astype(p_ref.dtype)
            inv_ref[rs, :] = jnp.where(
                head_slot == 0, inv[0], jnp.where(head_slot == 1, inv[1], jnp.where(head_slot == 2, inv[2], inv[3])))

    def values(u, p_ref, inv_ref):
        tiles = []
        for cb in window(u):
            even = vz_ref[rows_of(cb), 0:LANES]
            odd = vz_ref[rows_of(cb), LANES:2 * LANES]
            tiles += [jnp.concatenate([even, zv], axis=1), jnp.concatenate([odd, zv], axis=1),
                      jnp.concatenate([zv, even], axis=1), jnp.concatenate([zv, odd], axis=1)]
        o = jnp.dot(p_ref[...], jnp.concatenate(tiles, axis=0), preferred_element_type=jnp.float32)
        o_ref[rows_of(u), :] = (o * inv_ref[...]).astype(o_ref.dtype)

    s_refs, p_refs, inv_refs = (s0_ref, s1_ref), (p0_ref, p1_ref), (inv0_ref, inv1_ref)
    scores(0, s0_ref)
    p1_ref[...] = jnp.zeros_like(p1_ref)
    inv1_ref[...] = jnp.zeros_like(inv1_ref)

    def body(t, carry):
        for k in range(UNITS_PER_ITER):
            u = t * UNITS_PER_ITER + k
            cur, other = k % 2, (k + 1) % 2
            values(jnp.maximum(u - 1, 0), p_refs[other], inv_refs[other])
            scores(jnp.minimum(u + 1, nb - 1), s_refs[other])
            softmax(u, s_refs[cur], p_refs[cur], inv_refs[cur])
        return carry

    lax.fori_loop(0, nb // UNITS_PER_ITER, body, 0)
    values(nb - 1, p1_ref, inv1_ref)


def _band_attn(q, kt, vz, sink, layer, batch, seq, casts):
    assert UNITS_PER_ITER % 2 == 0 and (seq // BLOCK) % UNITS_PER_ITER == 0
    score_w = 3 * GROUP * BLOCK
    bf = jnp.bfloat16
    cast_in, cast_out, cast_shapes = [], [], []
    for w, lyr in casts:
        rows, cols = w.shape[1] // batch, w.shape[2]
        assert w.shape[1] % batch == 0 and rows % (2 * SUBLANES) == 0
        cast_in.append(pl.BlockSpec((None, rows, cols), functools.partial(lambda b, g, l: (l, b, 0), l=lyr)))
        cast_out.append(pl.BlockSpec((rows, cols), lambda b, g: (b, 0)))
        cast_shapes.append(jax.ShapeDtypeStruct(w.shape[1:], bf))
    tile = pl.BlockSpec((seq, GROUP_W), lambda b, g: (b, g))
    outs = pl.pallas_call(
        functools.partial(_band_attn_kernel, layer=layer, n_cast=len(casts)),
        out_shape=[jax.ShapeDtypeStruct(q.shape, bf)] + cast_shapes,
        grid=(batch, N_KV_HEADS),
        in_specs=[
            pl.BlockSpec(memory_space=pltpu.SMEM),
            tile,
            pl.BlockSpec((HEAD_DIM, seq), lambda b, g: (g, b)),
            tile,
        ] + cast_in,
        out_specs=[tile] + cast_out,
        scratch_shapes=[
            pltpu.VMEM((BLOCK, score_w), jnp.float32), pltpu.VMEM((BLOCK, score_w), jnp.float32),
            pltpu.VMEM((BLOCK, score_w), jnp.bfloat16), pltpu.VMEM((BLOCK, score_w), jnp.bfloat16),
            pltpu.VMEM((BLOCK, GROUP_W), jnp.float32), pltpu.VMEM((BLOCK, GROUP_W), jnp.float32),
        ],
        compiler_params=pltpu.CompilerParams(
            dimension_semantics=("arbitrary", "arbitrary"), vmem_limit_bytes=VMEM_LIMIT),
        name="band_attn",
    )(sink, q, kt, vz, *[w for w, _ in casts])
    return outs[0], outs[1:]


def _mix_tail_kernel(x_ref, a_ref, cb_ref, u_ref, up_ref, un_ref, sga_ref, sgb_ref,
                     cw_ref, wa_ref, wc_ref, wo_ref, g_ref, o_ref, *, pos_blocks):
    i = pl.program_id(0)
    tm = x_ref.shape[0]
    u = u_ref[...].astype(jnp.float32)
    first = (i % pos_blocks) == 0
    last = (i % pos_blocks) == pos_blocks - 1
    above = jnp.where(first, 0.0, up_ref[SUBLANES - 1:SUBLANES, :].astype(jnp.float32))
    below = jnp.where(last, 0.0, un_ref[0:1, :].astype(jnp.float32))
    row8 = lax.broadcasted_iota(jnp.int32, (SUBLANES, 1), 0)
    down = pltpu.roll(u, 1, axis=0)
    up = pltpu.roll(u, tm - 1, axis=0)
    u_before = jnp.concatenate([jnp.where(row8 == 0, above, down[:SUBLANES]), down[SUBLANES:]], axis=0)
    u_after = jnp.concatenate(
        [up[:-SUBLANES], jnp.where(row8 == SUBLANES - 1, below, up[-SUBLANES:])], axis=0)
    cw = cw_ref[...]
    conv = u_before * cw[0:1, :] + u * cw[1:2, :] + u_after * cw[2:3, :]
    y_attn = jnp.dot(a_ref[...], wa_ref[...], preferred_element_type=jnp.float32)
    c = cb_ref[...] * conv.astype(jnp.bfloat16)
    y_conv = jnp.dot(c, wc_ref[...], preferred_element_type=jnp.float32)
    mix = sga_ref[...].astype(jnp.float32) * y_attn + sgb_ref[...].astype(jnp.float32) * y_conv
    y = jnp.dot(mix.astype(jnp.bfloat16), wo_ref[...], preferred_element_type=jnp.float32)
    o_ref[...] = x_ref[...] + _rms_norm_f32(y, g_ref[...])


def _mix_tail(x, attn, cb, u, sga, sgb, conv_w, wa, wc, wo, g, layer, seq):
    n = x.shape[0]
    tm = MIX_TILE
    pos_blocks = seq // tm
    halo_blocks = tm // SUBLANES
    row = lambda i: (i, 0)
    prev = lambda i: (jnp.maximum(i * halo_blocks - 1, 0), 0)
    nxt = lambda i: (jnp.minimum((i + 1) * halo_blocks, n // SUBLANES - 1), 0)
    wspec = _resident((D_MODEL, D_MODEL))
    tile = pl.BlockSpec((tm, D_MODEL), row)
    return pl.pallas_call(
        functools.partial(_mix_tail_kernel, pos_blocks=pos_blocks),
        out_shape=jax.ShapeDtypeStruct(x.shape, x.dtype),
        grid=(n // tm,),
        in_specs=[
            tile, tile, tile, tile,
            pl.BlockSpec((SUBLANES, D_MODEL), prev), pl.BlockSpec((SUBLANES, D_MODEL), nxt),
            tile, tile,
            pl.BlockSpec((None, CONV_K, D_MODEL), lambda i: (layer, 0, 0)),
            wspec, wspec, wspec,
            pl.BlockSpec((None, 1, D_MODEL), lambda i: (layer, 0, 0)),
        ],
        out_specs=tile,
        compiler_params=pltpu.CompilerParams(
            dimension_semantics=("arbitrary",), vmem_limit_bytes=VMEM_LIMIT),
        name="mix_tail",
    )(x, attn, cb, u, u, u, sga, sgb, conv_w, wa, wc, wo, g)


def _ffn_kernel(x_ref, gpre_ref, wg_ref, wu_ref, wd_ref, gpost_ref, o_ref, act_ref):
    subs = [slice(r, r + ROW_SUB) for r in range(0, x_ref.shape[0], ROW_SUB)]
    hs = [_rms_norm_f32(x_ref[rs, :], gpre_ref[...]).astype(jnp.bfloat16) for rs in subs]
    for rs, h in zip(subs, hs):
        for j in range(0, D_FF, COL_CHUNK):
            w = min(COL_CHUNK, D_FF - j)
            gate = jnp.dot(h, wg_ref[:, j:j + w], preferred_element_type=jnp.float32)
            up = jnp.dot(h, wu_ref[:, j:j + w], preferred_element_type=jnp.float32)
            act_ref[rs, j:j + w] = (gate * jax.nn.sigmoid(gate) * up).astype(act_ref.dtype)
        f = jnp.dot(act_ref[rs, :], wd_ref[...], preferred_element_type=jnp.float32)
        o_ref[rs, :] = x_ref[rs, :] + _rms_norm_f32(f, gpost_ref[...])


def _ffn(x, gpre, wg, wu, wd, gpost, layer):
    n = x.shape[0]
    tm = FFN_TILE
    tile = pl.BlockSpec((tm, D_MODEL), lambda i: (i, 0))
    gspec = pl.BlockSpec((None, 1, D_MODEL), lambda i: (layer, 0, 0))
    return pl.pallas_call(
        _ffn_kernel,
        out_shape=jax.ShapeDtypeStruct(x.shape, x.dtype),
        grid=(n // tm,),
        in_specs=[
            tile, gspec,
            _resident((D_MODEL, D_FF)), _resident((D_MODEL, D_FF)), _resident((D_FF, D_MODEL)),
            gspec,
        ],
        out_specs=tile,
        scratch_shapes=[pltpu.VMEM((tm, D_FF), jnp.bfloat16)],
        compiler_params=pltpu.CompilerParams(
            dimension_semantics=("arbitrary",), vmem_limit_bytes=VMEM_LIMIT),
        name="ffn",
    )(x, gpre, wg, wu, wd, gpost)


def _rope_lane_tables(seq):
    pos = jnp.arange(seq, dtype=jnp.float32)
    inv = jnp.power(ROPE_THETA, -jnp.arange(0, ROT_DIM, 2, dtype=jnp.float32) / ROT_DIM)
    ang = pos[:, None] * inv[None, :]
    cos, sin = jnp.cos(ang), jnp.sin(ang)
    ones = jnp.ones((seq, HEAD_DIM - ROT_DIM), jnp.float32)
    zeros_r = jnp.zeros((seq, HEAD_DIM - ROT_DIM), jnp.float32)
    zeros_h = jnp.zeros((seq, ROT_HALF), jnp.float32)
    c = jnp.concatenate([cos, cos, ones], axis=1)
    sa = jnp.concatenate([-sin, zeros_h, zeros_r], axis=1)
    sb = jnp.concatenate([zeros_h, sin, zeros_r], axis=1)
    rep = LANES // HEAD_DIM
    return tuple(jnp.tile(t, (1, rep)) for t in (c, sa, sb)), (cos.T, sin.T)


def kernel(x, g_pre_mix, w_in, attn_sink, conv_w, w_attn_proj, w_conv_proj, w_out,
           g_post_mix, g_pre_ffn, w_gate, w_up, w_down, g_post_ffn):
    batch, seq, d = x.shape
    depth = w_in.shape[0]
    assert d == D_MODEL and seq % max(IN_TILE, MIX_TILE, FFN_TILE) == 0 and seq % BLOCK == 0
    gains = [g.reshape(depth, 1, D_MODEL) for g in (g_pre_mix, g_post_mix, g_pre_ffn, g_post_ffn)]
    tabs, tabs_t = _rope_lane_tables(seq)
    xf = x.reshape(batch * seq, d)
    w_in_bf = w_in[0].astype(jnp.bfloat16)
    for layer in range(depth):
        q, kt, vz, cb, u, sga, sgb = _in_proj(xf, gains[0], w_in_bf, layer, tabs, tabs_t, seq)
        casts = [(w, layer) for w in (w_attn_proj, w_conv_proj, w_out, w_gate, w_up, w_down)]
        if layer + 1 < depth:
            casts.append((w_in, layer + 1))
        attn, cast = _band_attn(q, kt, vz, attn_sink, layer, batch, seq, casts)
        wa, wc, wo, wg, wu, wd = cast[:6]
        if layer + 1 < depth:
            w_in_bf = cast[6]
        xf = _mix_tail(xf, attn, cb, u, sga, sgb, conv_w, wa, wc, wo, gains[1], layer, seq)
        xf = _ffn(xf, gains[2], wg, wu, wd, gains[3], layer)
    return xf.reshape(batch, seq, d)
```

```python
import functools

import jax
import jax.numpy as jnp
from jax import lax
from jax.experimental import pallas as pl
from jax.experimental.pallas import tpu as pltpu

D_MODEL = 1024
N_HEADS = 16
HEAD_DIM = 64
N_KV_HEADS = 4
GROUP = N_HEADS // N_KV_HEADS
ROT_DIM = HEAD_DIM // 4
ROT_HALF = ROT_DIM // 2
ROPE_THETA = 500000.0
WINDOW = 128
BLOCK = 128
CONV_K = 3
D_FF = 2816
Q_W = N_HEADS * HEAD_DIM
KV_W = N_KV_HEADS * HEAD_DIM
GROUP_W = GROUP * HEAD_DIM
IN_COLS = Q_W + 2 * KV_W + 3 * D_MODEL + 2 * D_MODEL
RMS_EPS = 1e-6
NEG_INF = -1e30
LOG2E = 1.4426950408889634

LANES = 128
SUBLANES = 8
IN_TILE = 1024
MIX_TILE = 1024
FFN_TILE = 1024
COL_CHUNK = 512
ROW_SUB = 256
SLAB = 16
UNITS_PER_ITER = 2
VMEM_LIMIT = 56 * 1024 * 1024
RESIDENT = pl.Buffered(1)

_OFF_Q = 0
_OFF_K = Q_W
_OFF_V = Q_W + KV_W
_OFF_CB = Q_W + 2 * KV_W
_OFF_CC = _OFF_CB + D_MODEL
_OFF_CX = _OFF_CC + D_MODEL
_OFF_GA = _OFF_CX + D_MODEL
_OFF_GB = _OFF_GA + D_MODEL


def _rms_norm_f32(x, g):
    return x * lax.rsqrt(jnp.mean(x * x, axis=-1, keepdims=True) + RMS_EPS) * g


def _rope_chunk(z, c, sa, sb):
    return z * c + pltpu.roll(z, LANES - ROT_HALF, axis=1) * sa + pltpu.roll(z, ROT_HALF, axis=1) * sb


def _resident(shape):
    return pl.BlockSpec(shape, lambda *_: (0,) * len(shape), pipeline_mode=RESIDENT)


def _in_proj_kernel(x_ref, g_ref, w_ref, c_ref, sa_ref, sb_ref, ct_ref, st_ref,
                    q_ref, kt_ref, vz_ref, cb_ref, u_ref, sga_ref, sgb_ref):
    scale = HEAD_DIM ** -0.5 * LOG2E
    subs = [slice(r, r + ROW_SUB) for r in range(0, x_ref.shape[0], ROW_SUB)]
    hs = [_rms_norm_f32(x_ref[rs, :], g_ref[...]).astype(jnp.bfloat16) for rs in subs]
    for rs, h in zip(subs, hs):
        def proj(off, width):
            return jnp.dot(h, w_ref[:, off:off + width], preferred_element_type=jnp.float32)

        c, sa, sb = c_ref[rs, :], sa_ref[rs, :], sb_ref[rs, :]

        def rope(z):
            return jnp.concatenate(
                [_rope_chunk(z[:, j:j + LANES], c, sa, sb) for j in range(0, z.shape[1], LANES)], axis=1)

        for j in range(0, Q_W, COL_CHUNK):
            q_ref[rs, j:j + COL_CHUNK] = (rope(proj(_OFF_Q + j, COL_CHUNK)) * scale).astype(q_ref.dtype)

        kt = proj(_OFF_K, KV_W).T
        ct, st = ct_ref[:, rs], st_ref[:, rs]
        pieces = []
        for g in range(N_KV_HEADS):
            base = g * HEAD_DIM
            x1 = kt[base:base + ROT_HALF]
            x2 = kt[base + ROT_HALF:base + ROT_DIM]
            pieces += [x1 * ct - x2 * st, x2 * ct + x1 * st, kt[base + ROT_DIM:base + HEAD_DIM]]
        kt_ref[:, rs] = jnp.concatenate(pieces, axis=0).astype(kt_ref.dtype)

        v = proj(_OFF_V, KV_W)
        zero = jnp.zeros((v.shape[0], HEAD_DIM), jnp.float32)
        halves = []
        for g in range(N_KV_HEADS):
            vg = v[:, g * HEAD_DIM:(g + 1) * HEAD_DIM]
            halves += [vg, zero, zero, vg]
        vz_ref[rs, :] = jnp.concatenate(halves, axis=1).astype(vz_ref.dtype)

        for j in range(0, D_MODEL, COL_CHUNK):
            cols = slice(j, j + COL_CHUNK)
            cb_ref[rs, cols] = proj(_OFF_CB + j, COL_CHUNK).astype(cb_ref.dtype)
            u = proj(_OFF_CC + j, COL_CHUNK) * proj(_OFF_CX + j, COL_CHUNK)
            u_ref[rs, cols] = u.astype(u_ref.dtype)
            sga_ref[rs, cols] = jax.nn.sigmoid(proj(_OFF_GA + j, COL_CHUNK)).astype(sga_ref.dtype)
            sgb_ref[rs, cols] = jax.nn.sigmoid(proj(_OFF_GB + j, COL_CHUNK)).astype(sgb_ref.dtype)


def _in_proj(x, g, w_in, layer, tabs, tabs_t, seq):
    n = x.shape[0]
    tm = IN_TILE
    pos_blocks = seq // tm
    row = lambda i: (i, 0)
    tab = lambda i: (i % pos_blocks, 0)
    tab_t = lambda i: (0, i % pos_blocks)
    bf = jnp.bfloat16
    out_shape = (
        jax.ShapeDtypeStruct((n, Q_W), bf), jax.ShapeDtypeStruct((KV_W, n), bf),
        jax.ShapeDtypeStruct((n, N_KV_HEADS * GROUP_W), bf), jax.ShapeDtypeStruct((n, D_MODEL), bf),
        jax.ShapeDtypeStruct((n, D_MODEL), bf), jax.ShapeDtypeStruct((n, D_MODEL), bf),
        jax.ShapeDtypeStruct((n, D_MODEL), bf))
    return pl.pallas_call(
        _in_proj_kernel,
        out_shape=out_shape,
        grid=(n // tm,),
        in_specs=[
            pl.BlockSpec((tm, D_MODEL), row),
            pl.BlockSpec((None, 1, D_MODEL), lambda i: (layer, 0, 0)),
            _resident((D_MODEL, IN_COLS)),
            pl.BlockSpec((tm, LANES), tab), pl.BlockSpec((tm, LANES), tab), pl.BlockSpec((tm, LANES), tab),
            pl.BlockSpec((ROT_HALF, tm), tab_t), pl.BlockSpec((ROT_HALF, tm), tab_t),
        ],
        out_specs=(
            pl.BlockSpec((tm, Q_W), row), pl.BlockSpec((KV_W, tm), lambda i: (0, i)),
            pl.BlockSpec((tm, N_KV_HEADS * GROUP_W), row),
            pl.BlockSpec((tm, D_MODEL), row), pl.BlockSpec((tm, D_MODEL), row),
            pl.BlockSpec((tm, D_MODEL), row), pl.BlockSpec((tm, D_MODEL), row)),
        compiler_params=pltpu.CompilerParams(
            dimension_semantics=("arbitrary",), vmem_limit_bytes=VMEM_LIMIT),
        name="in_proj",
    )(x, g, w_in, *tabs, *tabs_t)


def _band_attn_kernel(sink_ref, q_ref, kt_ref, vz_ref, *rest, layer, n_cast):
    cast_src = rest[:n_cast]
    o_ref = rest[n_cast]
    cast_dst = rest[n_cast + 1:2 * n_cast + 1]
    s0_ref, s1_ref, p0_ref, p1_ref, inv0_ref, inv1_ref = rest[2 * n_cast + 1:]
    g = pl.program_id(1)

    @pl.when(g == 0)
    def _():
        for src, dst in zip(cast_src, cast_dst):
            dst[...] = src[...].astype(dst.dtype)

    nb = q_ref.shape[0] // BLOCK
    n_slabs = BLOCK // SLAB
    r = lax.broadcasted_iota(jnp.int32, (SLAB, BLOCK), 0)
    c = lax.broadcasted_iota(jnp.int32, (SLAB, BLOCK), 1)
    left_ok = [c >= r + t * SLAB for t in range(n_slabs)]
    right_ok = [c <= r + t * SLAB for t in range(n_slabs)]
    head_slot = lax.broadcasted_iota(jnp.int32, (SLAB, GROUP_W), 1) // HEAD_DIM
    sinks = [sink_ref[layer, g * GROUP + h] * LOG2E for h in range(GROUP)]
    zk = jnp.zeros((HEAD_DIM, BLOCK), jnp.bfloat16)
    zv = jnp.zeros((BLOCK, LANES), jnp.bfloat16)

    def rows_of(u):
        return pl.ds(pl.multiple_of(u * BLOCK, BLOCK), BLOCK)

    def window(u):
        return (jnp.maximum(u - 1, 0), u, jnp.minimum(u + 1, nb - 1))

    def scores(u, s_ref):
        tiles = []
        for cb in window(u):
            kt = kt_ref[:, rows_of(cb)]
            tiles.append(jnp.concatenate(
                [jnp.concatenate([kt if a == h else zk for a in range(GROUP)], axis=1) for h in range(GROUP)],
                axis=0))
        s_ref[...] = jnp.dot(q_ref[rows_of(u), :], jnp.concatenate(tiles, axis=1),
                             preferred_element_type=jnp.float32)

    def softmax(u, s_ref, p_ref, inv_ref):
        has_left, has_right = u > 0, u < nb - 1
        for t in range(n_slabs):
            rs = slice(t * SLAB, (t + 1) * SLAB)
            lmask = left_ok[t] & has_left
            rmask = right_ok[t] & has_right
            p = [[None] * GROUP for _ in range(3)]
            inv = []
            for h in range(GROUP):
                col = lambda w: slice((w * GROUP + h) * BLOCK, (w * GROUP + h + 1) * BLOCK)
                a0 = jnp.where(lmask, s_ref[rs, col(0)], NEG_INF)
                a1 = s_ref[rs, col(1)]
                a2 = jnp.where(rmask, s_ref[rs, col(2)], NEG_INF)
                m = jnp.maximum(jnp.max(jnp.maximum(jnp.maximum(a0, a1), a2), axis=-1, keepdims=True), sinks[h])
                e0, e1, e2 = jnp.exp2(a0 - m), jnp.exp2(a1 - m), jnp.exp2(a2 - m)
                denom = jnp.sum(e0 + e1 + e2, axis=-1, keepdims=True) + jnp.exp2(sinks[h] - m)
                p[0][h], p[1][h], p[2][h] = e0, e1, e2
                inv.append(1.0 / denom)
            p_ref[rs, :] = jnp.concatenate(
                [p[w][h] for w in range(3) for h in range(GROUP)], axis=1).astype(p_ref.dtype)
            inv_ref[rs, :] = jnp.where(
                head_slot == 0, inv[0], jnp.where(head_slot == 1, inv[1], jnp.where(head_slot == 2, inv[2], inv[3])))

    def values(u, p_ref, inv_ref):
        tiles = []
        for cb in window(u):
            even = vz_ref[rows_of(cb), 0:LANES]
            odd = vz_ref[rows_of(cb), LANES:2 * LANES]
            tiles += [jnp.concatenate([even, zv], axis=1), jnp.concatenate([odd, zv], axis=1),
                      jnp.concatenate([zv, even], axis=1), jnp.concatenate([zv, odd], axis=1)]
        o = jnp.dot(p_ref[...], jnp.concatenate(tiles, axis=0), preferred_element_type=jnp.float32)
        o_ref[rows_of(u), :] = (o * inv_ref[...]).astype(o_ref.dtype)

    s_refs, p_refs, inv_refs = (s0_ref, s1_ref), (p0_ref, p1_ref), (inv0_ref, inv1_ref)
    scores(0, s0_ref)
    p1_ref[...] = jnp.zeros_like(p1_ref)
    inv1_ref[...] = jnp.zeros_like(inv1_ref)

    def body(t, carry):
        for k in range(UNITS_PER_ITER):
            u = t * UNITS_PER_ITER + k
            cur, other = k % 2, (k + 1) % 2
            values(jnp.maximum(u - 1, 0), p_refs[other], inv_refs[other])
            scores(jnp.minimum(u + 1, nb - 1), s_refs[other])
            softmax(u, s_refs[cur], p_refs[cur], inv_refs[cur])
        return carry

    lax.fori_loop(0, nb // UNITS_PER_ITER, body, 0)
    values(nb - 1, p1_ref, inv1_ref)


def _band_attn(q, kt, vz, sink, layer, batch, seq, casts):
    assert UNITS_PER_ITER % 2 == 0 and (seq // BLOCK) % UNITS_PER_ITER == 0
    score_w = 3 * GROUP * BLOCK
    bf = jnp.bfloat16
    cast_in, cast_out, cast_shapes = [], [], []
    for w, lyr in casts:
        rows, cols = w.shape[1] // batch, w.shape[2]
        assert w.shape[1] % batch == 0 and rows % (2 * SUBLANES) == 0
        cast_in.append(pl.BlockSpec((None, rows, cols), functools.partial(lambda b, g, l: (l, b, 0), l=lyr)))
        cast_out.append(pl.BlockSpec((rows, cols), lambda b, g: (b, 0)))
        cast_shapes.append(jax.ShapeDtypeStruct(w.shape[1:], bf))
    tile = pl.BlockSpec((seq, GROUP_W), lambda b, g: (b, g))
    outs = pl.pallas_call(
        functools.partial(_band_attn_kernel, layer=layer, n_cast=len(casts)),
        out_shape=[jax.ShapeDtypeStruct(q.shape, bf)] + cast_shapes,
        grid=(batch, N_KV_HEADS),
        in_specs=[
            pl.BlockSpec(memory_space=pltpu.SMEM),
            tile,
            pl.BlockSpec((HEAD_DIM, seq), lambda b, g: (g, b)),
            tile,
        ] + cast_in,
        out_specs=[tile] + cast_out,
        scratch_shapes=[
            pltpu.VMEM((BLOCK, score_w), jnp.float32), pltpu.VMEM((BLOCK, score_w), jnp.float32),
            pltpu.VMEM((BLOCK, score_w), jnp.bfloat16), pltpu.VMEM((BLOCK, score_w), jnp.bfloat16),
            pltpu.VMEM((BLOCK, GROUP_W), jnp.float32), pltpu.VMEM((BLOCK, GROUP_W), jnp.float32),
        ],
        compiler_params=pltpu.CompilerParams(
            dimension_semantics=("arbitrary", "arbitrary"), vmem_limit_bytes=VMEM_LIMIT),
        name="band_attn",
    )(sink, q, kt, vz, *[w for w, _ in casts])
    return outs[0], outs[1:]


def _mix_tail_kernel(x_ref, a_ref, cb_ref, u_ref, up_ref, un_ref, sga_ref, sgb_ref,
                     cw_ref, wa_ref, wc_ref, wo_ref, g_ref, o_ref, *, pos_blocks):
    i = pl.program_id(0)
    tm = x_ref.shape[0]
    u = u_ref[...].astype(jnp.float32)
    first = (i % pos_blocks) == 0
    last = (i % pos_blocks) == pos_blocks - 1
    above = jnp.where(first, 0.0, up_ref[SUBLANES - 1:SUBLANES, :].astype(jnp.float32))
    below = jnp.where(last, 0.0, un_ref[0:1, :].astype(jnp.float32))
    row8 = lax.broadcasted_iota(jnp.int32, (SUBLANES, 1), 0)
    down = pltpu.roll(u, 1, axis=0)
    up = pltpu.roll(u, tm - 1, axis=0)
    u_before = jnp.concatenate([jnp.where(row8 == 0, above, down[:SUBLANES]), down[SUBLANES:]], axis=0)
    u_after = jnp.concatenate(
        [up[:-SUBLANES], jnp.where(row8 == SUBLANES - 1, below, up[-SUBLANES:])], axis=0)
    cw = cw_ref[...]
    conv = u_before * cw[0:1, :] + u * cw[1:2, :] + u_after * cw[2:3, :]
    c = cb_ref[...] * conv.astype(jnp.bfloat16)
    for r0 in range(0, tm, ROW_SUB):
        rs = slice(r0, r0 + ROW_SUB)
        y_attn = jnp.dot(a_ref[rs, :], wa_ref[...], preferred_element_type=jnp.float32)
        y_conv = jnp.dot(c[rs], wc_ref[...], preferred_element_type=jnp.float32)
        mix = sga_ref[rs, :].astype(jnp.float32) * y_attn + sgb_ref[rs, :].astype(jnp.float32) * y_conv
        y = jnp.dot(mix.astype(jnp.bfloat16), wo_ref[...], preferred_element_type=jnp.float32)
        o_ref[rs, :] = x_ref[rs, :] + _rms_norm_f32(y, g_ref[...])


def _mix_tail(x, attn, cb, u, sga, sgb, conv_w, wa, wc, wo, g, layer, seq):
    n = x.shape[0]
    tm = MIX_TILE
    pos_blocks = seq // tm
    halo_blocks = tm // SUBLANES
    row = lambda i: (i, 0)
    prev = lambda i: (jnp.maximum(i * halo_blocks - 1, 0), 0)
    nxt = lambda i: (jnp.minimum((i + 1) * halo_blocks, n // SUBLANES - 1), 0)
    wspec = _resident((D_MODEL, D_MODEL))
    tile = pl.BlockSpec((tm, D_MODEL), row)
    return pl.pallas_call(
        functools.partial(_mix_tail_kernel, pos_blocks=pos_blocks),
        out_shape=jax.ShapeDtypeStruct(x.shape, x.dtype),
        grid=(n // tm,),
        in_specs=[
            tile, tile, tile, tile,
            pl.BlockSpec((SUBLANES, D_MODEL), prev), pl.BlockSpec((SUBLANES, D_MODEL), nxt),
            tile, tile,
            pl.BlockSpec((None, CONV_K, D_MODEL), lambda i: (layer, 0, 0)),
            wspec, wspec, wspec,
            pl.BlockSpec((None, 1, D_MODEL), lambda i: (layer, 0, 0)),
        ],
        out_specs=tile,
        compiler_params=pltpu.CompilerParams(
            dimension_semantics=("arbitrary",), vmem_limit_bytes=VMEM_LIMIT),
        name="mix_tail",
    )(x, attn, cb, u, u, u, sga, sgb, conv_w, wa, wc, wo, g)


def _ffn_kernel(x_ref, gpre_ref, wg_ref, wu_ref, wd_ref, gpost_ref, o_ref, act_ref):
    subs = [slice(r, r + ROW_SUB) for r in range(0, x_ref.shape[0], ROW_SUB)]
    hs = [_rms_norm_f32(x_ref[rs, :], gpre_ref[...]).astype(jnp.bfloat16) for rs in subs]
    for rs, h in zip(subs, hs):
        for j in range(0, D_FF, COL_CHUNK):
            w = min(COL_CHUNK, D_FF - j)
            gate = jnp.dot(h, wg_ref[:, j:j + w], preferred_element_type=jnp.float32)
            up = jnp.dot(h, wu_ref[:, j:j + w], preferred_element_type=jnp.float32)
            act_ref[rs, j:j + w] = (gate * jax.nn.sigmoid(gate) * up).astype(act_ref.dtype)
        f = jnp.dot(act_ref[rs, :], wd_ref[...], preferred_element_type=jnp.float32)
        o_ref[rs, :] = x_ref[rs, :] + _rms_norm_f32(f, gpost_ref[...])


def _ffn(x, gpre, wg, wu, wd, gpost, layer):
    n = x.shape[0]
    tm = FFN_TILE
    tile = pl.BlockSpec((tm, D_MODEL), lambda i: (i, 0))
    gspec = pl.BlockSpec((None, 1, D_MODEL), lambda i: (layer, 0, 0))
    return pl.pallas_call(
        _ffn_kernel,
        out_shape=jax.ShapeDtypeStruct(x.shape, x.dtype),
        grid=(n // tm,),
        in_specs=[
            tile, gspec,
            _resident((D_MODEL, D_FF)), _resident((D_MODEL, D_FF)), _resident((D_FF, D_MODEL)),
            gspec,
        ],
        out_specs=tile,
        scratch_shapes=[pltpu.VMEM((tm, D_FF), jnp.bfloat16)],
        compiler_params=pltpu.CompilerParams(
            dimension_semantics=("arbitrary",), vmem_limit_bytes=VMEM_LIMIT),
        name="ffn",
    )(x, gpre, wg, wu, wd, gpost)


def _rope_lane_tables(seq):
    pos = jnp.arange(seq, dtype=jnp.float32)
    inv = jnp.power(ROPE_THETA, -jnp.arange(0, ROT_DIM, 2, dtype=jnp.float32) / ROT_DIM)
    ang = pos[:, None] * inv[None, :]
    cos, sin = jnp.cos(ang), jnp.sin(ang)
    ones = jnp.ones((seq, HEAD_DIM - ROT_DIM), jnp.float32)
    zeros_r = jnp.zeros((seq, HEAD_DIM - ROT_DIM), jnp.float32)
    zeros_h = jnp.zeros((seq, ROT_HALF), jnp.float32)
    c = jnp.concatenate([cos, cos, ones], axis=1)
    sa = jnp.concatenate([-sin, zeros_h, zeros_r], axis=1)
    sb = jnp.concatenate([zeros_h, sin, zeros_r], axis=1)
    rep = LANES // HEAD_DIM
    return tuple(jnp.tile(t, (1, rep)) for t in (c, sa, sb)), (cos.T, sin.T)


def kernel(x, g_pre_mix, w_in, attn_sink, conv_w, w_attn_proj, w_conv_proj, w_out,
           g_post_mix, g_pre_ffn, w_gate, w_up, w_down, g_post_ffn):
    batch, seq, d = x.shape
    depth = w_in.shape[0]
    assert d == D_MODEL and seq % max(IN_TILE, MIX_TILE, FFN_TILE) == 0 and seq % BLOCK == 0
    gains = [g.reshape(depth, 1, D_MODEL) for g in (g_pre_mix, g_post_mix, g_pre_ffn, g_post_ffn)]
    tabs, tabs_t = _rope_lane_tables(seq)
    xf = x.reshape(batch * seq, d)
    w_in_bf = w_in[0].astype(jnp.bfloat16)
    for layer in range(depth):
        q, kt, vz, cb, u, sga, sgb = _in_proj(xf, gains[0], w_in_bf, layer, tabs, tabs_t, seq)
        casts = [(w, layer) for w in (w_attn_proj, w_conv_proj, w_out, w_gate, w_up, w_down)]
        if layer + 1 < depth:
            casts.append((w_in, layer + 1))
        attn, cast = _band_attn(q, kt, vz, attn_sink, layer, batch, seq, casts)
        wa, wc, wo, wg, wu, wd = cast[:6]
        if layer + 1 < depth:
            w_in_bf = cast[6]
        xf = _mix_tail(xf, attn, cb, u, sga, sgb, conv_w, wa, wc, wo, gains[1], layer, seq)
        xf = _ffn(xf, gains[2], wg, wu, wd, gains[3], layer)
    return xf.reshape(batch, seq, d)
```

```python
import functools

import jax
import jax.numpy as jnp
from jax import lax
from jax.experimental import pallas as pl
from jax.experimental.pallas import tpu as pltpu

D_MODEL = 1024
N_HEADS = 16
HEAD_DIM = 64
N_KV_HEADS = 4
GROUP = N_HEADS // N_KV_HEADS
ROT_DIM = HEAD_DIM // 4
ROT_HALF = ROT_DIM // 2
ROPE_THETA = 500000.0
WINDOW = 128
BLOCK = 128
CONV_K = 3
D_FF = 2816
Q_W = N_HEADS * HEAD_DIM
KV_W = N_KV_HEADS * HEAD_DIM
GROUP_W = GROUP * HEAD_DIM
IN_COLS = Q_W + 2 * KV_W + 3 * D_MODEL + 2 * D_MODEL
RMS_EPS = 1e-6
NEG_INF = -1e30
LOG2E = 1.4426950408889634

LANES = 128
SUBLANES = 8
IN_TILE = 1024
MIX_TILE = 1024
FFN_TILE = 1024
COL_CHUNK = 512
ROW_SUB = 256
SLAB = 16
VMEM_LIMIT = 56 * 1024 * 1024
RESIDENT = pl.Buffered(1)

_OFF_Q = 0
_OFF_K = Q_W
_OFF_V = Q_W + KV_W
_OFF_CB = Q_W + 2 * KV_W
_OFF_CC = _OFF_CB + D_MODEL
_OFF_CX = _OFF_CC + D_MODEL
_OFF_GA = _OFF_CX + D_MODEL
_OFF_GB = _OFF_GA + D_MODEL


def _rms_norm_f32(x, g):
    return x * lax.rsqrt(jnp.mean(x * x, axis=-1, keepdims=True) + RMS_EPS) * g


def _rope_chunk(z, c, sa, sb):
    return z * c + pltpu.roll(z, LANES - ROT_HALF, axis=1) * sa + pltpu.roll(z, ROT_HALF, axis=1) * sb


def _resident(shape):
    return pl.BlockSpec(shape, lambda *_: (0,) * len(shape), pipeline_mode=RESIDENT)


def _in_proj_kernel(x_ref, g_ref, w_ref, c_ref, sa_ref, sb_ref, ct_ref, st_ref,
                    q_ref, kt_ref, vz_ref, cb_ref, u_ref, sga_ref, sgb_ref):
    scale = HEAD_DIM ** -0.5 * LOG2E
    subs = [slice(r, r + ROW_SUB) for r in range(0, x_ref.shape[0], ROW_SUB)]
    hs = [_rms_norm_f32(x_ref[rs, :], g_ref[...]).astype(jnp.bfloat16) for rs in subs]
    for rs, h in zip(subs, hs):
        def proj(off, width):
            return jnp.dot(h, w_ref[:, off:off + width], preferred_element_type=jnp.float32)

        c, sa, sb = c_ref[rs, :], sa_ref[rs, :], sb_ref[rs, :]

        def rope(z):
            return jnp.concatenate(
                [_rope_chunk(z[:, j:j + LANES], c, sa, sb) for j in range(0, z.shape[1], LANES)], axis=1)

        for j in range(0, Q_W, COL_CHUNK):
            q_ref[rs, j:j + COL_CHUNK] = (rope(proj(_OFF_Q + j, COL_CHUNK)) * scale).astype(q_ref.dtype)

        kt = proj(_OFF_K, KV_W).T
        ct, st = ct_ref[:, rs], st_ref[:, rs]
        pieces = []
        for g in range(N_KV_HEADS):
            base = g * HEAD_DIM
            x1 = kt[base:base + ROT_HALF]
            x2 = kt[base + ROT_HALF:base + ROT_DIM]
            pieces += [x1 * ct - x2 * st, x2 * ct + x1 * st, kt[base + ROT_DIM:base + HEAD_DIM]]
        kt_ref[:, rs] = jnp.concatenate(pieces, axis=0).astype(kt_ref.dtype)

        v = proj(_OFF_V, KV_W)
        zero = jnp.zeros((v.shape[0], HEAD_DIM), jnp.float32)
        halves = []
        for g in range(N_KV_HEADS):
            vg = v[:, g * HEAD_DIM:(g + 1) * HEAD_DIM]
            halves += [vg, zero, zero, vg]
        vz_ref[rs, :] = jnp.concatenate(halves, axis=1).astype(vz_ref.dtype)

        for j in range(0, D_MODEL, COL_CHUNK):
            cols = slice(j, j + COL_CHUNK)
            cb_ref[rs, cols] = proj(_OFF_CB + j, COL_CHUNK).astype(cb_ref.dtype)
            u = proj(_OFF_CC + j, COL_CHUNK) * proj(_OFF_CX + j, COL_CHUNK)
            u_ref[rs, cols] = u.astype(u_ref.dtype)
            sga_ref[rs, cols] = jax.nn.sigmoid(proj(_OFF_GA + j, COL_CHUNK)).astype(sga_ref.dtype)
            sgb_ref[rs, cols] = jax.nn.sigmoid(proj(_OFF_GB + j, COL_CHUNK)).astype(sgb_ref.dtype)


def _in_proj(x, g, w_in, layer, tabs, tabs_t, seq):
    n = x.shape[0]
    tm = IN_TILE
    pos_blocks = seq // tm
    row = lambda i: (i, 0)
    tab = lambda i: (i % pos_blocks, 0)
    tab_t = lambda i: (0, i % pos_blocks)
    bf = jnp.bfloat16
    out_shape = (
        jax.ShapeDtypeStruct((n, Q_W), bf), jax.ShapeDtypeStruct((KV_W, n), bf),
        jax.ShapeDtypeStruct((n, N_KV_HEADS * GROUP_W), bf), jax.ShapeDtypeStruct((n, D_MODEL), bf),
        jax.ShapeDtypeStruct((n, D_MODEL), bf), jax.ShapeDtypeStruct((n, D_MODEL), bf),
        jax.ShapeDtypeStruct((n, D_MODEL), bf))
    return pl.pallas_call(
        _in_proj_kernel,
        out_shape=out_shape,
        grid=(n // tm,),
        in_specs=[
            pl.BlockSpec((tm, D_MODEL), row),
            pl.BlockSpec((None, 1, D_MODEL), lambda i: (layer, 0, 0)),
            _resident((D_MODEL, IN_COLS)),
            pl.BlockSpec((tm, LANES), tab), pl.BlockSpec((tm, LANES), tab), pl.BlockSpec((tm, LANES), tab),
            pl.BlockSpec((ROT_HALF, tm), tab_t), pl.BlockSpec((ROT_HALF, tm), tab_t),
        ],
        out_specs=(
            pl.BlockSpec((tm, Q_W), row), pl.BlockSpec((KV_W, tm), lambda i: (0, i)),
            pl.BlockSpec((tm, N_KV_HEADS * GROUP_W), row),
            pl.BlockSpec((tm, D_MODEL), row), pl.BlockSpec((tm, D_MODEL), row),
            pl.BlockSpec((tm, D_MODEL), row), pl.BlockSpec((tm, D_MODEL), row)),
        compiler_params=pltpu.CompilerParams(
            dimension_semantics=("arbitrary",), vmem_limit_bytes=VMEM_LIMIT),
        name="in_proj",
    )(x, g, w_in, *tabs, *tabs_t)


def _band_attn_kernel(sink_ref, q_ref, kt_ref, vz_ref, *rest, layer, n_cast):
    cast_src = rest[:n_cast]
    o_ref = rest[n_cast]
    cast_dst = rest[n_cast + 1:2 * n_cast + 1]
    s0_ref, p1_ref, inv1_ref = rest[2 * n_cast + 1:]
    g = pl.program_id(1)

    @pl.when(g == 0)
    def _():
        for src, dst in zip(cast_src, cast_dst):
            dst[...] = src[...].astype(dst.dtype)

    nb = q_ref.shape[0] // BLOCK
    n_slabs = BLOCK // SLAB
    r = lax.broadcasted_iota(jnp.int32, (SLAB, BLOCK), 0)
    c = lax.broadcasted_iota(jnp.int32, (SLAB, BLOCK), 1)
    left_ok = [c >= r + t * SLAB for t in range(n_slabs)]
    right_ok = [c <= r + t * SLAB for t in range(n_slabs)]
    head_slot = lax.broadcasted_iota(jnp.int32, (SLAB, GROUP_W), 1) // HEAD_DIM
    sinks = [sink_ref[layer, g * GROUP + h] * LOG2E for h in range(GROUP)]
    zk = jnp.zeros((HEAD_DIM, BLOCK), jnp.bfloat16)
    zv = jnp.zeros((BLOCK, LANES), jnp.bfloat16)

    def by_head(cols):
        return jnp.where(head_slot == 0, cols[0],
                         jnp.where(head_slot == 1, cols[1], jnp.where(head_slot == 2, cols[2], cols[3])))

    sink_lanes = by_head(sinks)

    def rows_of(u):
        return pl.ds(pl.multiple_of(u * BLOCK, BLOCK), BLOCK)

    def window(u):
        return (jnp.maximum(u - 1, 0), u, jnp.minimum(u + 1, nb - 1))

    def scores(u):
        tiles = []
        for cb in window(u):
            kt = kt_ref[:, rows_of(cb)]
            tiles.append(jnp.concatenate(
                [jnp.concatenate([kt if a == h else zk for a in range(GROUP)], axis=1) for h in range(GROUP)],
                axis=0))
        return jnp.dot(q_ref[rows_of(u), :], jnp.concatenate(tiles, axis=1), preferred_element_type=jnp.float32)

    def softmax(u, s):
        has_left, has_right = u > 0, u < nb - 1
        p_rows, inv_rows = [], []
        for t in range(n_slabs):
            rs = slice(t * SLAB, (t + 1) * SLAB)
            lmask = left_ok[t] & has_left
            rmask = right_ok[t] & has_right
            p = [[None] * GROUP for _ in range(3)]
            ms, ls = [], []
            for h in range(GROUP):
                col = lambda w: slice((w * GROUP + h) * BLOCK, (w * GROUP + h + 1) * BLOCK)
                a0 = jnp.where(lmask, s[rs, col(0)], NEG_INF)
                a1 = s[rs, col(1)]
                a2 = jnp.where(rmask, s[rs, col(2)], NEG_INF)
                m = jnp.maximum(jnp.max(jnp.maximum(jnp.maximum(a0, a1), a2), axis=-1, keepdims=True), sinks[h])
                e0, e1, e2 = jnp.exp2(a0 - m), jnp.exp2(a1 - m), jnp.exp2(a2 - m)
                ms.append(m)
                ls.append(jnp.sum(e0 + e1 + e2, axis=-1, keepdims=True))
                p[0][h], p[1][h], p[2][h] = e0, e1, e2
            p_rows.append(jnp.concatenate(
                [p[w][h] for w in range(3) for h in range(GROUP)], axis=1).astype(jnp.bfloat16))
            inv_rows.append(1.0 / (by_head(ls) + jnp.exp2(sink_lanes - by_head(ms))))
        return jnp.concatenate(p_rows, axis=0), jnp.concatenate(inv_rows, axis=0)

    def values(u, p, inv):
        tiles = []
        for cb in window(u):
            even = vz_ref[rows_of(cb), 0:LANES]
            odd = vz_ref[rows_of(cb), LANES:2 * LANES]
            tiles += [jnp.concatenate([even, zv], axis=1), jnp.concatenate([odd, zv], axis=1),
                      jnp.concatenate([zv, even], axis=1), jnp.concatenate([zv, odd], axis=1)]
        o = jnp.dot(p, jnp.concatenate(tiles, axis=0), preferred_element_type=jnp.float32)
        o_ref[rows_of(u), :] = (o * inv).astype(o_ref.dtype)

    s0_ref[...] = scores(0)
    p1_ref[...] = jnp.zeros_like(p1_ref)
    inv1_ref[...] = jnp.zeros_like(inv1_ref)

    def body(t, carry):
        a = 2 * t
        b = a + 1
        values(jnp.maximum(a - 1, 0), p1_ref[...], inv1_ref[...])
        s_b = scores(b)
        p_a, inv_a = softmax(a, s0_ref)
        values(a, p_a, inv_a)
        s0_ref[...] = scores(jnp.minimum(a + 2, nb - 1))
        p1_ref[...], inv1_ref[...] = softmax(b, s_b)
        return carry

    lax.fori_loop(0, nb // 2, body, 0)
    values(nb - 1, p1_ref[...], inv1_ref[...])


def _band_attn(q, kt, vz, sink, layer, batch, seq, casts):
    assert (seq // BLOCK) % 2 == 0
    score_w = 3 * GROUP * BLOCK
    bf = jnp.bfloat16
    cast_in, cast_out, cast_shapes = [], [], []
    for w, lyr in casts:
        rows, cols = w.shape[1] // batch, w.shape[2]
        assert w.shape[1] % batch == 0 and rows % (2 * SUBLANES) == 0
        cast_in.append(pl.BlockSpec((None, rows, cols), functools.partial(lambda b, g, l: (l, b, 0), l=lyr)))
        cast_out.append(pl.BlockSpec((rows, cols), lambda b, g: (b, 0)))
        cast_shapes.append(jax.ShapeDtypeStruct(w.shape[1:], bf))
    tile = pl.BlockSpec((seq, GROUP_W), lambda b, g: (b, g))
    outs = pl.pallas_call(
        functools.partial(_band_attn_kernel, layer=layer, n_cast=len(casts)),
        out_shape=[jax.ShapeDtypeStruct(q.shape, bf)] + cast_shapes,
        grid=(batch, N_KV_HEADS),
        in_specs=[
            pl.BlockSpec(memory_space=pltpu.SMEM),
            tile,
            pl.BlockSpec((HEAD_DIM, seq), lambda b, g: (g, b)),
            tile,
        ] + cast_in,
        out_specs=[tile] + cast_out,
        scratch_shapes=[
            pltpu.VMEM((BLOCK, score_w), jnp.float32),
            pltpu.VMEM((BLOCK, score_w), jnp.bfloat16),
            pltpu.VMEM((BLOCK, GROUP_W), jnp.float32),
        ],
        compiler_params=pltpu.CompilerParams(
            dimension_semantics=("arbitrary", "arbitrary"), vmem_limit_bytes=VMEM_LIMIT),
        name="band_attn",
    )(sink, q, kt, vz, *[w for w, _ in casts])
    return outs[0], outs[1:]


def _mix_tail_kernel(x_ref, a_ref, cb_ref, u_ref, up_ref, un_ref, sga_ref, sgb_ref,
                     cw_ref, wa_ref, wc_ref, wo_ref, g_ref, o_ref, *, pos_blocks):
    i = pl.program_id(0)
    tm = x_ref.shape[0]
    u = u_ref[...].astype(jnp.float32)
    first = (i % pos_blocks) == 0
    last = (i % pos_blocks) == pos_blocks - 1
    above = jnp.where(first, 0.0, up_ref[SUBLANES - 1:SUBLANES, :].astype(jnp.float32))
    below = jnp.where(last, 0.0, un_ref[0:1, :].astype(jnp.float32))
    row8 = lax.broadcasted_iota(jnp.int32, (SUBLANES, 1), 0)
    down = pltpu.roll(u, 1, axis=0)
    up = pltpu.roll(u, tm - 1, axis=0)
    u_before = jnp.concatenate([jnp.where(row8 == 0, above, down[:SUBLANES]), down[SUBLANES:]], axis=0)
    u_after = jnp.concatenate(
        [up[:-SUBLANES], jnp.where(row8 == SUBLANES - 1, below, up[-SUBLANES:])], axis=0)
    cw = cw_ref[...]
    conv = u_before * cw[0:1, :] + u * cw[1:2, :] + u_after * cw[2:3, :]
    c = cb_ref[...] * conv.astype(jnp.bfloat16)
    for r0 in range(0, tm, ROW_SUB):
        rs = slice(r0, r0 + ROW_SUB)
        y_attn = jnp.dot(a_ref[rs, :], wa_ref[...], preferred_element_type=jnp.float32)
        y_conv = jnp.dot(c[rs], wc_ref[...], preferred_element_type=jnp.float32)
        mix = sga_ref[rs, :].astype(jnp.float32) * y_attn + sgb_ref[rs, :].astype(jnp.float32) * y_conv
        y = jnp.dot(mix.astype(jnp.bfloat16), wo_ref[...], preferred_element_type=jnp.float32)
        o_ref[rs, :] = x_ref[rs, :] + _rms_norm_f32(y, g_ref[...])


def _mix_tail(x, attn, cb, u, sga, sgb, conv_w, wa, wc, wo, g, layer, seq):
    n = x.shape[0]
    tm = MIX_TILE
    pos_blocks = seq // tm
    halo_blocks = tm // SUBLANES
    row = lambda i: (i, 0)
    prev = lambda i: (jnp.maximum(i * halo_blocks - 1, 0), 0)
    nxt = lambda i: (jnp.minimum((i + 1) * halo_blocks, n // SUBLANES - 1), 0)
    wspec = _resident((D_MODEL, D_MODEL))
    tile = pl.BlockSpec((tm, D_MODEL), row)
    return pl.pallas_call(
        functools.partial(_mix_tail_kernel, pos_blocks=pos_blocks),
        out_shape=jax.ShapeDtypeStruct(x.shape, x.dtype),
        grid=(n // tm,),
        in_specs=[
            tile, tile, tile, tile,
            pl.BlockSpec((SUBLANES, D_MODEL), prev), pl.BlockSpec((SUBLANES, D_MODEL), nxt),
            tile, tile,
            pl.BlockSpec((None, CONV_K, D_MODEL), lambda i: (layer, 0, 0)),
            wspec, wspec, wspec,
            pl.BlockSpec((None, 1, D_MODEL), lambda i: (layer, 0, 0)),
        ],
        out_specs=tile,
        compiler_params=pltpu.CompilerParams(
            dimension_semantics=("arbitrary",), vmem_limit_bytes=VMEM_LIMIT),
        name="mix_tail",
    )(x, attn, cb, u, u, u, sga, sgb, conv_w, wa, wc, wo, g)


def _ffn_kernel(x_ref, gpre_ref, wg_ref, wu_ref, wd_ref, gpost_ref, o_ref, act_ref):
    subs = [slice(r, r + ROW_SUB) for r in range(0, x_ref.shape[0], ROW_SUB)]
    hs = [_rms_norm_f32(x_ref[rs, :], gpre_ref[...]).astype(jnp.bfloat16) for rs in subs]
    for rs, h in zip(subs, hs):
        for j in range(0, D_FF, COL_CHUNK):
            w = min(COL_CHUNK, D_FF - j)
            gate = jnp.dot(h, wg_ref[:, j:j + w], preferred_element_type=jnp.float32)
            up = jnp.dot(h, wu_ref[:, j:j + w], preferred_element_type=jnp.float32)
            act_ref[rs, j:j + w] = (gate * jax.nn.sigmoid(gate) * up).astype(act_ref.dtype)
        f = jnp.dot(act_ref[rs, :], wd_ref[...], preferred_element_type=jnp.float32)
        o_ref[rs, :] = x_ref[rs, :] + _rms_norm_f32(f, gpost_ref[...])


def _ffn(x, gpre, wg, wu, wd, gpost, layer):
    n = x.shape[0]
    tm = FFN_TILE
    tile = pl.BlockSpec((tm, D_MODEL), lambda i: (i, 0))
    gspec = pl.BlockSpec((None, 1, D_MODEL), lambda i: (layer, 0, 0))
    return pl.pallas_call(
        _ffn_kernel,
        out_shape=jax.ShapeDtypeStruct(x.shape, x.dtype),
        grid=(n // tm,),
        in_specs=[
            tile, gspec,
            _resident((D_MODEL, D_FF)), _resident((D_MODEL, D_FF)), _resident((D_FF, D_MODEL)),
            gspec,
        ],
        out_specs=tile,
        scratch_shapes=[pltpu.VMEM((tm, D_FF), jnp.bfloat16)],
        compiler_params=pltpu.CompilerParams(
            dimension_semantics=("arbitrary",), vmem_limit_bytes=VMEM_LIMIT),
        name="ffn",
    )(x, gpre, wg, wu, wd, gpost)


def _rope_lane_tables(seq):
    pos = jnp.arange(seq, dtype=jnp.float32)
    inv = jnp.power(ROPE_THETA, -jnp.arange(0, ROT_DIM, 2, dtype=jnp.float32) / ROT_DIM)
    ang = pos[:, None] * inv[None, :]
    cos, sin = jnp.cos(ang), jnp.sin(ang)
    ones = jnp.ones((seq, HEAD_DIM - ROT_DIM), jnp.float32)
    zeros_r = jnp.zeros((seq, HEAD_DIM - ROT_DIM), jnp.float32)
    zeros_h = jnp.zeros((seq, ROT_HALF), jnp.float32)
    c = jnp.concatenate([cos, cos, ones], axis=1)
    sa = jnp.concatenate([-sin, zeros_h, zeros_r], axis=1)
    sb = jnp.concatenate([zeros_h, sin, zeros_r], axis=1)
    rep = LANES // HEAD_DIM
    return tuple(jnp.tile(t, (1, rep)) for t in (c, sa, sb)), (cos.T, sin.T)


def kernel(x, g_pre_mix, w_in, attn_sink, conv_w, w_attn_proj, w_conv_proj, w_out,
           g_post_mix, g_pre_ffn, w_gate, w_up, w_down, g_post_ffn):
    batch, seq, d = x.shape
    depth = w_in.shape[0]
    assert d == D_MODEL and seq % max(IN_TILE, MIX_TILE, FFN_TILE) == 0 and seq % BLOCK == 0
    gains = [g.reshape(depth, 1, D_MODEL) for g in (g_pre_mix, g_post_mix, g_pre_ffn, g_post_ffn)]
    tabs, tabs_t = _rope_lane_tables(seq)
    xf = x.reshape(batch * seq, d)
    w_in_bf = w_in[0].astype(jnp.bfloat16)
    for layer in range(depth):
        q, kt, vz, cb, u, sga, sgb = _in_proj(xf, gains[0], w_in_bf, layer, tabs, tabs_t, seq)
        casts = [(w, layer) for w in (w_attn_proj, w_conv_proj, w_out, w_gate, w_up, w_down)]
        if layer + 1 < depth:
            casts.append((w_in, layer + 1))
        attn, cast = _band_attn(q, kt, vz, attn_sink, layer, batch, seq, casts)
        wa, wc, wo, wg, wu, wd = cast[:6]
        if layer + 1 < depth:
            w_in_bf = cast[6]
        xf = _mix_tail(xf, attn, cb, u, sga, sgb, conv_w, wa, wc, wo, gains[1], layer, seq)
        xf = _ffn(xf, gains[2], wg, wu, wd, gains[3], layer)
    return xf.reshape(batch, seq, d)
```

```python
import functools

import jax
import jax.numpy as jnp
from jax import lax
from jax.experimental import pallas as pl
from jax.experimental.pallas import tpu as pltpu

D_MODEL = 1024
N_HEADS = 16
HEAD_DIM = 64
N_KV_HEADS = 4
GROUP = N_HEADS // N_KV_HEADS
ROT_DIM = HEAD_DIM // 4
ROT_HALF = ROT_DIM // 2
ROPE_THETA = 500000.0
WINDOW = 128
BLOCK = 128
CONV_K = 3
D_FF = 2816
Q_W = N_HEADS * HEAD_DIM
KV_W = N_KV_HEADS * HEAD_DIM
GROUP_W = GROUP * HEAD_DIM
IN_COLS = Q_W + 2 * KV_W + 3 * D_MODEL + 2 * D_MODEL
RMS_EPS = 1e-6
NEG_INF = -1e30
LOG2E = 1.4426950408889634

LANES = 128
SUBLANES = 8
IN_TILE = 1024
MIX_TILE = 1024
FFN_TILE = 1024
COL_CHUNK = 512
ROW_SUB = 256
SLAB = 16
VMEM_LIMIT = 56 * 1024 * 1024
RESIDENT = pl.Buffered(1)

_OFF_Q = 0
_OFF_K = Q_W
_OFF_V = Q_W + KV_W
_OFF_CB = Q_W + 2 * KV_W
_OFF_CC = _OFF_CB + D_MODEL
_OFF_CX = _OFF_CC + D_MODEL
_OFF_GA = _OFF_CX + D_MODEL
_OFF_GB = _OFF_GA + D_MODEL


def _rms_norm_f32(x, g):
    return x * lax.rsqrt(jnp.mean(x * x, axis=-1, keepdims=True) + RMS_EPS) * g


def _rope_chunk(z, c, sa, sb):
    return z * c + pltpu.roll(z, LANES - ROT_HALF, axis=1) * sa + pltpu.roll(z, ROT_HALF, axis=1) * sb


def _resident(shape):
    return pl.BlockSpec(shape, lambda *_: (0,) * len(shape), pipeline_mode=RESIDENT)


def _in_proj_kernel(x_ref, g_ref, w_ref, c_ref, sa_ref, sb_ref, ct_ref, st_ref,
                    q_ref, kt_ref, vz_ref, cb_ref, u_ref, sga_ref, sgb_ref):
    scale = HEAD_DIM ** -0.5 * LOG2E
    subs = [slice(r, r + ROW_SUB) for r in range(0, x_ref.shape[0], ROW_SUB)]
    hs = [_rms_norm_f32(x_ref[rs, :], g_ref[...]).astype(jnp.bfloat16) for rs in subs]
    for rs, h in zip(subs, hs):
        def proj(off, width):
            return jnp.dot(h, w_ref[:, off:off + width], preferred_element_type=jnp.float32)

        c, sa, sb = c_ref[rs, :], sa_ref[rs, :], sb_ref[rs, :]

        def rope(z):
            return jnp.concatenate(
                [_rope_chunk(z[:, j:j + LANES], c, sa, sb) for j in range(0, z.shape[1], LANES)], axis=1)

        for j in range(0, Q_W, COL_CHUNK):
            q_ref[rs, j:j + COL_CHUNK] = (rope(proj(_OFF_Q + j, COL_CHUNK)) * scale).astype(q_ref.dtype)

        kt = proj(_OFF_K, KV_W).T
        ct, st = ct_ref[:, rs], st_ref[:, rs]
        pieces = []
        for g in range(N_KV_HEADS):
            base = g * HEAD_DIM
            x1 = kt[base:base + ROT_HALF]
            x2 = kt[base + ROT_HALF:base + ROT_DIM]
            pieces += [x1 * ct - x2 * st, x2 * ct + x1 * st, kt[base + ROT_DIM:base + HEAD_DIM]]
        kt_ref[:, rs] = jnp.concatenate(pieces, axis=0).astype(kt_ref.dtype)

        v = proj(_OFF_V, KV_W)
        zero = jnp.zeros((v.shape[0], HEAD_DIM), jnp.float32)
        halves = []
        for g in range(N_KV_HEADS):
            vg = v[:, g * HEAD_DIM:(g + 1) * HEAD_DIM]
            halves += [vg, zero, zero, vg]
        vz_ref[rs, :] = jnp.concatenate(halves, axis=1).astype(vz_ref.dtype)

        for j in range(0, D_MODEL, COL_CHUNK):
            cols = slice(j, j + COL_CHUNK)
            cb_ref[rs, cols] = proj(_OFF_CB + j, COL_CHUNK).astype(cb_ref.dtype)
            u = proj(_OFF_CC + j, COL_CHUNK) * proj(_OFF_CX + j, COL_CHUNK)
            u_ref[rs, cols] = u.astype(u_ref.dtype)
            sga_ref[rs, cols] = jax.nn.sigmoid(proj(_OFF_GA + j, COL_CHUNK)).astype(sga_ref.dtype)
            sgb_ref[rs, cols] = jax.nn.sigmoid(proj(_OFF_GB + j, COL_CHUNK)).astype(sgb_ref.dtype)


def _in_proj(x, g, w_in, layer, tabs, tabs_t, seq):
    n = x.shape[0]
    tm = IN_TILE
    pos_blocks = seq // tm
    row = lambda i: (i, 0)
    tab = lambda i: (i % pos_blocks, 0)
    tab_t = lambda i: (0, i % pos_blocks)
    bf = jnp.bfloat16
    out_shape = (
        jax.ShapeDtypeStruct((n, Q_W), bf), jax.ShapeDtypeStruct((KV_W, n), bf),
        jax.ShapeDtypeStruct((n, N_KV_HEADS * GROUP_W), bf), jax.ShapeDtypeStruct((n, D_MODEL), bf),
        jax.ShapeDtypeStruct((n, D_MODEL), bf), jax.ShapeDtypeStruct((n, D_MODEL), bf),
        jax.ShapeDtypeStruct((n, D_MODEL), bf))
    return pl.pallas_call(
        _in_proj_kernel,
        out_shape=out_shape,
        grid=(n // tm,),
        in_specs=[
            pl.BlockSpec((tm, D_MODEL), row),
            pl.BlockSpec((None, 1, D_MODEL), lambda i: (layer, 0, 0)),
            _resident((D_MODEL, IN_COLS)),
            pl.BlockSpec((tm, LANES), tab), pl.BlockSpec((tm, LANES), tab), pl.BlockSpec((tm, LANES), tab),
            pl.BlockSpec((ROT_HALF, tm), tab_t), pl.BlockSpec((ROT_HALF, tm), tab_t),
        ],
        out_specs=(
            pl.BlockSpec((tm, Q_W), row), pl.BlockSpec((KV_W, tm), lambda i: (0, i)),
            pl.BlockSpec((tm, N_KV_HEADS * GROUP_W), row),
            pl.BlockSpec((tm, D_MODEL), row), pl.BlockSpec((tm, D_MODEL), row),
            pl.BlockSpec((tm, D_MODEL), row), pl.BlockSpec((tm, D_MODEL), row)),
        compiler_params=pltpu.CompilerParams(
            dimension_semantics=("arbitrary",), vmem_limit_bytes=VMEM_LIMIT),
        name="in_proj",
    )(x, g, w_in, *tabs, *tabs_t)


def _band_attn_kernel(sink_ref, q_ref, kt_ref, vz_ref, *rest, layer, n_cast):
    cast_src = rest[:n_cast]
    o_ref = rest[n_cast]
    cast_dst = rest[n_cast + 1:2 * n_cast + 1]
    s0_ref, p1_ref, inv1_ref = rest[2 * n_cast + 1:]
    g = pl.program_id(1)

    @pl.when(g == 0)
    def _():
        for src, dst in zip(cast_src, cast_dst):
            dst[...] = src[...].astype(dst.dtype)

    nb = q_ref.shape[0] // BLOCK
    n_slabs = BLOCK // SLAB
    r = lax.broadcasted_iota(jnp.int32, (SLAB, BLOCK), 0)
    c = lax.broadcasted_iota(jnp.int32, (SLAB, BLOCK), 1)
    left_ok = [c >= r + t * SLAB for t in range(n_slabs)]
    right_ok = [c <= r + t * SLAB for t in range(n_slabs)]
    head_slot = lax.broadcasted_iota(jnp.int32, (SLAB, GROUP_W), 1) // HEAD_DIM
    sinks = [sink_ref[layer, g * GROUP + h] * LOG2E for h in range(GROUP)]
    zk = jnp.zeros((HEAD_DIM, BLOCK), jnp.bfloat16)
    zv = jnp.zeros((BLOCK, LANES), jnp.bfloat16)

    def by_head(cols):
        return jnp.where(head_slot == 0, cols[0],
                         jnp.where(head_slot == 1, cols[1], jnp.where(head_slot == 2, cols[2], cols[3])))

    sink_left = [(c == 0) & (r >= 1), c == 0]
    sink_right = (c == BLOCK - 1) & (r == 0)
    fill_left = [[jnp.where(msk, sinks[h], NEG_INF) for h in range(GROUP)] for msk in sink_left]
    fill_right = [jnp.where(sink_right, sinks[h], NEG_INF) for h in range(GROUP)]

    def rows_of(u):
        return pl.ds(pl.multiple_of(u * BLOCK, BLOCK), BLOCK)

    def window(u):
        return (jnp.maximum(u - 1, 0), u, jnp.minimum(u + 1, nb - 1))

    def scores(u):
        tiles = []
        for cb in window(u):
            kt = kt_ref[:, rows_of(cb)]
            tiles.append(jnp.concatenate(
                [jnp.concatenate([kt if a == h else zk for a in range(GROUP)], axis=1) for h in range(GROUP)],
                axis=0))
        return jnp.dot(q_ref[rows_of(u), :], jnp.concatenate(tiles, axis=1), preferred_element_type=jnp.float32)

    def softmax(u, s):
        has_left, has_right = u > 0, u < nb - 1
        p_rows, inv_rows = [], []
        for t in range(n_slabs):
            rs = slice(t * SLAB, (t + 1) * SLAB)
            lmask = left_ok[t] & has_left
            rmask = right_ok[t] & has_right
            p = [[None] * GROUP for _ in range(3)]
            ls = []
            first = min(t, 1)
            for h in range(GROUP):
                col = lambda w: slice((w * GROUP + h) * BLOCK, (w * GROUP + h + 1) * BLOCK)
                a0 = jnp.where(lmask, s[rs, col(0)], fill_left[first][h])
                a1 = s[rs, col(1)]
                a2 = jnp.where(rmask, s[rs, col(2)], fill_right[h] if t == 0 else NEG_INF)
                m = jnp.max(jnp.maximum(jnp.maximum(a0, a1), a2), axis=-1, keepdims=True)
                e0, e1, e2 = jnp.exp2(a0 - m), jnp.exp2(a1 - m), jnp.exp2(a2 - m)
                ls.append(jnp.sum(e0 + e1 + e2, axis=-1, keepdims=True))
                p[0][h] = jnp.where(sink_left[first], 0.0, e0)
                p[1][h] = e1
                p[2][h] = jnp.where(sink_right, 0.0, e2) if t == 0 else e2
            p_rows.append(jnp.concatenate(
                [p[w][h] for w in range(3) for h in range(GROUP)], axis=1).astype(jnp.bfloat16))
            inv_rows.append(1.0 / by_head(ls))
        return jnp.concatenate(p_rows, axis=0), jnp.concatenate(inv_rows, axis=0)

    def values(u, p, inv):
        tiles = []
        for cb in window(u):
            even = vz_ref[rows_of(cb), 0:LANES]
            odd = vz_ref[rows_of(cb), LANES:2 * LANES]
            tiles += [jnp.concatenate([even, zv], axis=1), jnp.concatenate([odd, zv], axis=1),
                      jnp.concatenate([zv, even], axis=1), jnp.concatenate([zv, odd], axis=1)]
        o = jnp.dot(p, jnp.concatenate(tiles, axis=0), preferred_element_type=jnp.float32)
        o_ref[rows_of(u), :] = (o * inv).astype(o_ref.dtype)

    s0_ref[...] = scores(0)
    p1_ref[...] = jnp.zeros_like(p1_ref)
    inv1_ref[...] = jnp.zeros_like(inv1_ref)

    def body(t, carry):
        a = 2 * t
        b = a + 1
        values(jnp.maximum(a - 1, 0), p1_ref[...], inv1_ref[...])
        s_b = scores(b)
        p_a, inv_a = softmax(a, s0_ref)
        values(a, p_a, inv_a)
        s0_ref[...] = scores(jnp.minimum(a + 2, nb - 1))
        p1_ref[...], inv1_ref[...] = softmax(b, s_b)
        return carry

    lax.fori_loop(0, nb // 2, body, 0)
    values(nb - 1, p1_ref[...], inv1_ref[...])


def _band_attn(q, kt, vz, sink, layer, batch, seq, casts):
    assert (seq // BLOCK) % 2 == 0
    score_w = 3 * GROUP * BLOCK
    bf = jnp.bfloat16
    cast_in, cast_out, cast_shapes = [], [], []
    for w, lyr in casts:
        rows, cols = w.shape[1] // batch, w.shape[2]
        assert w.shape[1] % batch == 0 and rows % (2 * SUBLANES) == 0
        cast_in.append(pl.BlockSpec((None, rows, cols), functools.partial(lambda b, g, l: (l, b, 0), l=lyr)))
        cast_out.append(pl.BlockSpec((rows, cols), lambda b, g: (b, 0)))
        cast_shapes.append(jax.ShapeDtypeStruct(w.shape[1:], bf))
    tile = pl.BlockSpec((seq, GROUP_W), lambda b, g: (b, g))
    outs = pl.pallas_call(
        functools.partial(_band_attn_kernel, layer=layer, n_cast=len(casts)),
        out_shape=[jax.ShapeDtypeStruct(q.shape, bf)] + cast_shapes,
        grid=(batch, N_KV_HEADS),
        in_specs=[
            pl.BlockSpec(memory_space=pltpu.SMEM),
            tile,
            pl.BlockSpec((HEAD_DIM, seq), lambda b, g: (g, b)),
            tile,
        ] + cast_in,
        out_specs=[tile] + cast_out,
        scratch_shapes=[
            pltpu.VMEM((BLOCK, score_w), jnp.float32),
            pltpu.VMEM((BLOCK, score_w), jnp.bfloat16),
            pltpu.VMEM((BLOCK, GROUP_W), jnp.float32),
        ],
        compiler_params=pltpu.CompilerParams(
            dimension_semantics=("arbitrary", "arbitrary"), vmem_limit_bytes=VMEM_LIMIT),
        name="band_attn",
    )(sink, q, kt, vz, *[w for w, _ in casts])
    return outs[0], outs[1:]


def _mix_tail_kernel(x_ref, a_ref, cb_ref, u_ref, up_ref, un_ref, sga_ref, sgb_ref,
                     cw_ref, wa_ref, wc_ref, wo_ref, g_ref, o_ref, *, pos_blocks):
    i = pl.program_id(0)
    tm = x_ref.shape[0]
    u = u_ref[...].astype(jnp.float32)
    first = (i % pos_blocks) == 0
    last = (i % pos_blocks) == pos_blocks - 1
    above = jnp.where(first, 0.0, up_ref[SUBLANES - 1:SUBLANES, :].astype(jnp.float32))
    below = jnp.where(last, 0.0, un_ref[0:1, :].astype(jnp.float32))
    row8 = lax.broadcasted_iota(jnp.int32, (SUBLANES, 1), 0)
    down = pltpu.roll(u, 1, axis=0)
    up = pltpu.roll(u, tm - 1, axis=0)
    u_before = jnp.concatenate([jnp.where(row8 == 0, above, down[:SUBLANES]), down[SUBLANES:]], axis=0)
    u_after = jnp.concatenate(
        [up[:-SUBLANES], jnp.where(row8 == SUBLANES - 1, below, up[-SUBLANES:])], axis=0)
    cw = cw_ref[...]
    conv = u_before * cw[0:1, :] + u * cw[1:2, :] + u_after * cw[2:3, :]
    c = cb_ref[...] * conv.astype(jnp.bfloat16)
    for r0 in range(0, tm, ROW_SUB):
        rs = slice(r0, r0 + ROW_SUB)
        y_attn = jnp.dot(a_ref[rs, :], wa_ref[...], preferred_element_type=jnp.float32)
        y_conv = jnp.dot(c[rs], wc_ref[...], preferred_element_type=jnp.float32)
        mix = sga_ref[rs, :].astype(jnp.float32) * y_attn + sgb_ref[rs, :].astype(jnp.float32) * y_conv
        y = jnp.dot(mix.astype(jnp.bfloat16), wo_ref[...], preferred_element_type=jnp.float32)
        o_ref[rs, :] = x_ref[rs, :] + _rms_norm_f32(y, g_ref[...])


def _mix_tail(x, attn, cb, u, sga, sgb, conv_w, wa, wc, wo, g, layer, seq):
    n = x.shape[0]
    tm = MIX_TILE
    pos_blocks = seq // tm
    halo_blocks = tm // SUBLANES
    row = lambda i: (i, 0)
    prev = lambda i: (jnp.maximum(i * halo_blocks - 1, 0), 0)
    nxt = lambda i: (jnp.minimum((i + 1) * halo_blocks, n // SUBLANES - 1), 0)
    wspec = _resident((D_MODEL, D_MODEL))
    tile = pl.BlockSpec((tm, D_MODEL), row)
    return pl.pallas_call(
        functools.partial(_mix_tail_kernel, pos_blocks=pos_blocks),
        out_shape=jax.ShapeDtypeStruct(x.shape, x.dtype),
        grid=(n // tm,),
        in_specs=[
            tile, tile, tile, tile,
            pl.BlockSpec((SUBLANES, D_MODEL), prev), pl.BlockSpec((SUBLANES, D_MODEL), nxt),
            tile, tile,
            pl.BlockSpec((None, CONV_K, D_MODEL), lambda i: (layer, 0, 0)),
            wspec, wspec, wspec,
            pl.BlockSpec((None, 1, D_MODEL), lambda i: (layer, 0, 0)),
        ],
        out_specs=tile,
        compiler_params=pltpu.CompilerParams(
            dimension_semantics=("arbitrary",), vmem_limit_bytes=VMEM_LIMIT),
        name="mix_tail",
    )(x, attn, cb, u, u, u, sga, sgb, conv_w, wa, wc, wo, g)


def _ffn_kernel(x_ref, gpre_ref, wg_ref, wu_ref, wd_ref, gpost_ref, o_ref, act_ref):
    subs = [slice(r, r + ROW_SUB) for r in range(0, x_ref.shape[0], ROW_SUB)]
    hs = [_rms_norm_f32(x_ref[rs, :], gpre_ref[...]).astype(jnp.bfloat16) for rs in subs]
    for rs, h in zip(subs, hs):
        for j in range(0, D_FF, COL_CHUNK):
            w = min(COL_CHUNK, D_FF - j)
            gate = jnp.dot(h, wg_ref[:, j:j + w], preferred_element_type=jnp.float32)
            up = jnp.dot(h, wu_ref[:, j:j + w], preferred_element_type=jnp.float32)
            act_ref[rs, j:j + w] = (gate * jax.nn.sigmoid(gate) * up).astype(act_ref.dtype)
        f = jnp.dot(act_ref[rs, :], wd_ref[...], preferred_element_type=jnp.float32)
        o_ref[rs, :] = x_ref[rs, :] + _rms_norm_f32(f, gpost_ref[...])


def _ffn(x, gpre, wg, wu, wd, gpost, layer):
    n = x.shape[0]
    tm = FFN_TILE
    tile = pl.BlockSpec((tm, D_MODEL), lambda i: (i, 0))
    gspec = pl.BlockSpec((None, 1, D_MODEL), lambda i: (layer, 0, 0))
    return pl.pallas_call(
        _ffn_kernel,
        out_shape=jax.ShapeDtypeStruct(x.shape, x.dtype),
        grid=(n // tm,),
        in_specs=[
            tile, gspec,
            _resident((D_MODEL, D_FF)), _resident((D_MODEL, D_FF)), _resident((D_FF, D_MODEL)),
            gspec,
        ],
        out_specs=tile,
        scratch_shapes=[pltpu.VMEM((tm, D_FF), jnp.bfloat16)],
        compiler_params=pltpu.CompilerParams(
            dimension_semantics=("arbitrary",), vmem_limit_bytes=VMEM_LIMIT),
        name="ffn",
    )(x, gpre, wg, wu, wd, gpost)


def _rope_lane_tables(seq):
    pos = jnp.arange(seq, dtype=jnp.float32)
    inv = jnp.power(ROPE_THETA, -jnp.arange(0, ROT_DIM, 2, dtype=jnp.float32) / ROT_DIM)
    ang = pos[:, None] * inv[None, :]
    cos, sin = jnp.cos(ang), jnp.sin(ang)
    ones = jnp.ones((seq, HEAD_DIM - ROT_DIM), jnp.float32)
    zeros_r = jnp.zeros((seq, HEAD_DIM - ROT_DIM), jnp.float32)
    zeros_h = jnp.zeros((seq, ROT_HALF), jnp.float32)
    c = jnp.concatenate([cos, cos, ones], axis=1)
    sa = jnp.concatenate([-sin, zeros_h, zeros_r], axis=1)
    sb = jnp.concatenate([zeros_h, sin, zeros_r], axis=1)
    rep = LANES // HEAD_DIM
    return tuple(jnp.tile(t, (1, rep)) for t in (c, sa, sb)), (cos.T, sin.T)


def kernel(x, g_pre_mix, w_in, attn_sink, conv_w, w_attn_proj, w_conv_proj, w_out,
           g_post_mix, g_pre_ffn, w_gate, w_up, w_down, g_post_ffn):
    batch, seq, d = x.shape
    depth = w_in.shape[0]
    assert d == D_MODEL and seq % max(IN_TILE, MIX_TILE, FFN_TILE) == 0 and seq % BLOCK == 0
    gains = [g.reshape(depth, 1, D_MODEL) for g in (g_pre_mix, g_post_mix, g_pre_ffn, g_post_ffn)]
    tabs, tabs_t = _rope_lane_tables(seq)
    xf = x.reshape(batch * seq, d)
    w_in_bf = w_in[0].astype(jnp.bfloat16)
    for layer in range(depth):
        q, kt, vz, cb, u, sga, sgb = _in_proj(xf, gains[0], w_in_bf, layer, tabs, tabs_t, seq)
        casts = [(w, layer) for w in (w_attn_proj, w_conv_proj, w_out, w_gate, w_up, w_down)]
        if layer + 1 < depth:
            casts.append((w_in, layer + 1))
        attn, cast = _band_attn(q, kt, vz, attn_sink, layer, batch, seq, casts)
        wa, wc, wo, wg, wu, wd = cast[:6]
        if layer + 1 < depth:
            w_in_bf = cast[6]
        xf = _mix_tail(xf, attn, cb, u, sga, sgb, conv_w, wa, wc, wo, gains[1], layer, seq)
        xf = _ffn(xf, gains[2], wg, wu, wd, gains[3], layer)
    return xf.reshape(batch, seq, d)
```

```python
import functools

import jax
import jax.numpy as jnp
from jax import lax
from jax.experimental import pallas as pl
from jax.experimental.pallas import tpu as pltpu

D_MODEL = 1024
N_HEADS = 16
HEAD_DIM = 64
N_KV_HEADS = 4
GROUP = N_HEADS // N_KV_HEADS
ROT_DIM = HEAD_DIM // 4
ROT_HALF = ROT_DIM // 2
ROPE_THETA = 500000.0
WINDOW = 128
BLOCK = 128
CONV_K = 3
D_FF = 2816
Q_W = N_HEADS * HEAD_DIM
KV_W = N_KV_HEADS * HEAD_DIM
GROUP_W = GROUP * HEAD_DIM
IN_COLS = Q_W + 2 * KV_W + 3 * D_MODEL + 2 * D_MODEL
RMS_EPS = 1e-6
NEG_INF = -1e30
LOG2E = 1.4426950408889634

LANES = 128
SUBLANES = 8
IN_TILE = 1024
MIX_TILE = 1024
FFN_TILE = 1024
COL_CHUNK = 512
ROW_SUB = 256
SLAB = 16
VMEM_LIMIT = 56 * 1024 * 1024
RESIDENT = pl.Buffered(1)

_OFF_Q = 0
_OFF_K = Q_W
_OFF_V = Q_W + KV_W
_OFF_CB = Q_W + 2 * KV_W
_OFF_CC = _OFF_CB + D_MODEL
_OFF_CX = _OFF_CC + D_MODEL
_OFF_GA = _OFF_CX + D_MODEL
_OFF_GB = _OFF_GA + D_MODEL


def _rms_norm_f32(x, g):
    return x * lax.rsqrt(jnp.mean(x * x, axis=-1, keepdims=True) + RMS_EPS) * g


def _rope_chunk(z, c, sa, sb):
    return z * c + pltpu.roll(z, LANES - ROT_HALF, axis=1) * sa + pltpu.roll(z, ROT_HALF, axis=1) * sb


def _resident(shape):
    return pl.BlockSpec(shape, lambda *_: (0,) * len(shape), pipeline_mode=RESIDENT)


def _in_proj_kernel(x_ref, g_ref, w_ref, c_ref, sa_ref, sb_ref, ct_ref, st_ref,
                    q_ref, kt_ref, vz_ref, cb_ref, u_ref, sga_ref, sgb_ref):
    scale = HEAD_DIM ** -0.5 * LOG2E
    subs = [slice(r, r + ROW_SUB) for r in range(0, x_ref.shape[0], ROW_SUB)]
    hs = [_rms_norm_f32(x_ref[rs, :], g_ref[...]).astype(jnp.bfloat16) for rs in subs]
    for rs, h in zip(subs, hs):
        def proj(off, width):
            return jnp.dot(h, w_ref[:, off:off + width], preferred_element_type=jnp.float32)

        c, sa, sb = c_ref[rs, :], sa_ref[rs, :], sb_ref[rs, :]

        def rope(z):
            return jnp.concatenate(
                [_rope_chunk(z[:, j:j + LANES], c, sa, sb) for j in range(0, z.shape[1], LANES)], axis=1)

        for j in range(0, Q_W, COL_CHUNK):
            q_ref[rs, j:j + COL_CHUNK] = (rope(proj(_OFF_Q + j, COL_CHUNK)) * scale).astype(q_ref.dtype)

        kt = proj(_OFF_K, KV_W).T
        ct, st = ct_ref[:, rs], st_ref[:, rs]
        pieces = []
        for g in range(N_KV_HEADS):
            base = g * HEAD_DIM
            x1 = kt[base:base + ROT_HALF]
            x2 = kt[base + ROT_HALF:base + ROT_DIM]
            pieces += [x1 * ct - x2 * st, x2 * ct + x1 * st, kt[base + ROT_DIM:base + HEAD_DIM]]
        kt_ref[:, rs] = jnp.concatenate(pieces, axis=0).astype(kt_ref.dtype)

        v = proj(_OFF_V, KV_W)
        zero = jnp.zeros((v.shape[0], HEAD_DIM), jnp.float32)
        halves = []
        for g in range(N_KV_HEADS):
            vg = v[:, g * HEAD_DIM:(g + 1) * HEAD_DIM]
            halves += [vg, zero, zero, vg]
        vz_ref[rs, :] = jnp.concatenate(halves, axis=1).astype(vz_ref.dtype)

        for j in range(0, D_MODEL, COL_CHUNK):
            cols = slice(j, j + COL_CHUNK)
            cb_ref[rs, cols] = proj(_OFF_CB + j, COL_CHUNK).astype(cb_ref.dtype)
            u = proj(_OFF_CC + j, COL_CHUNK) * proj(_OFF_CX + j, COL_CHUNK)
            u_ref[rs, cols] = u.astype(u_ref.dtype)
            sga_ref[rs, cols] = jax.nn.sigmoid(proj(_OFF_GA + j, COL_CHUNK)).astype(sga_ref.dtype)
            sgb_ref[rs, cols] = jax.nn.sigmoid(proj(_OFF_GB + j, COL_CHUNK)).astype(sgb_ref.dtype)


def _in_proj(x, g, w_in, layer, tabs, tabs_t, seq):
    n = x.shape[0]
    tm = IN_TILE
    pos_blocks = seq // tm
    row = lambda i: (i, 0)
    tab = lambda i: (i % pos_blocks, 0)
    tab_t = lambda i: (0, i % pos_blocks)
    bf = jnp.bfloat16
    out_shape = (
        jax.ShapeDtypeStruct((n, Q_W), bf), jax.ShapeDtypeStruct((KV_W, n), bf),
        jax.ShapeDtypeStruct((n, N_KV_HEADS * GROUP_W), bf), jax.ShapeDtypeStruct((n, D_MODEL), bf),
        jax.ShapeDtypeStruct((n, D_MODEL), bf), jax.ShapeDtypeStruct((n, D_MODEL), bf),
        jax.ShapeDtypeStruct((n, D_MODEL), bf))
    return pl.pallas_call(
        _in_proj_kernel,
        out_shape=out_shape,
        grid=(n // tm,),
        in_specs=[
            pl.BlockSpec((tm, D_MODEL), row),
            pl.BlockSpec((None, 1, D_MODEL), lambda i: (layer, 0, 0)),
            _resident((D_MODEL, IN_COLS)),
            pl.BlockSpec((tm, LANES), tab), pl.BlockSpec((tm, LANES), tab), pl.BlockSpec((tm, LANES), tab),
            pl.BlockSpec((ROT_HALF, tm), tab_t), pl.BlockSpec((ROT_HALF, tm), tab_t),
        ],
        out_specs=(
            pl.BlockSpec((tm, Q_W), row), pl.BlockSpec((KV_W, tm), lambda i: (0, i)),
            pl.BlockSpec((tm, N_KV_HEADS * GROUP_W), row),
            pl.BlockSpec((tm, D_MODEL), row), pl.BlockSpec((tm, D_MODEL), row),
            pl.BlockSpec((tm, D_MODEL), row), pl.BlockSpec((tm, D_MODEL), row)),
        compiler_params=pltpu.CompilerParams(
            dimension_semantics=("arbitrary",), vmem_limit_bytes=VMEM_LIMIT),
        name="in_proj",
    )(x, g, w_in, *tabs, *tabs_t)


def _band_attn_kernel(sink_ref, q_ref, kt_ref, vz_ref, *rest, layer, n_cast):
    cast_src = rest[:n_cast]
    o_ref = rest[n_cast]
    cast_dst = rest[n_cast + 1:2 * n_cast + 1]
    s0_ref, p1_ref, inv1_ref = rest[2 * n_cast + 1:]
    g = pl.program_id(1)

    @pl.when(g == 0)
    def _():
        for src, dst in zip(cast_src, cast_dst):
            dst[...] = src[...].astype(dst.dtype)

    nb = q_ref.shape[0] // BLOCK
    n_slabs = BLOCK // SLAB
    r = lax.broadcasted_iota(jnp.int32, (SLAB, BLOCK), 0)
    c = lax.broadcasted_iota(jnp.int32, (SLAB, BLOCK), 1)
    left_ok = [c >= r + t * SLAB for t in range(n_slabs)]
    right_ok = [c <= r + t * SLAB for t in range(n_slabs)]
    head_slot = lax.broadcasted_iota(jnp.int32, (SLAB, GROUP_W), 1) // HEAD_DIM
    sinks = [sink_ref[layer, g * GROUP + h] * LOG2E for h in range(GROUP)]
    zk = jnp.zeros((HEAD_DIM, BLOCK), jnp.bfloat16)
    zv = jnp.zeros((BLOCK, LANES), jnp.bfloat16)

    def by_head(cols):
        return jnp.where(head_slot == 0, cols[0],
                         jnp.where(head_slot == 1, cols[1], jnp.where(head_slot == 2, cols[2], cols[3])))

    sink_left = [(c == 0) & (r >= 1), c == 0]
    sink_right = (c == BLOCK - 1) & (r == 0)
    fill_left = [[jnp.where(msk, sinks[h], NEG_INF) for h in range(GROUP)] for msk in sink_left]
    fill_right = [jnp.where(sink_right, sinks[h], NEG_INF) for h in range(GROUP)]

    def rows_of(u):
        return pl.ds(pl.multiple_of(u * BLOCK, BLOCK), BLOCK)

    def window(u):
        return (jnp.maximum(u - 1, 0), u, jnp.minimum(u + 1, nb - 1))

    def scores(u):
        tiles = []
        for cb in window(u):
            kt = kt_ref[:, rows_of(cb)]
            tiles.append(jnp.concatenate(
                [jnp.concatenate([kt if a == h else zk for a in range(GROUP)], axis=1) for h in range(GROUP)],
                axis=0))
        return jnp.dot(q_ref[rows_of(u), :], jnp.concatenate(tiles, axis=1), preferred_element_type=jnp.float32)

    def softmax(u, s):
        has_left, has_right = u > 0, u < nb - 1
        p_rows, inv_rows = [], []
        for t in range(n_slabs):
            rs = slice(t * SLAB, (t + 1) * SLAB)
            lmask = left_ok[t] & has_left
            rmask = right_ok[t] & has_right
            p = [[None] * GROUP for _ in range(3)]
            ls = []
            first = min(t, 1)
            for h in range(GROUP):
                col = lambda w: slice((w * GROUP + h) * BLOCK, (w * GROUP + h + 1) * BLOCK)
                a0 = jnp.where(lmask, s[rs, col(0)], fill_left[first][h])
                a1 = s[rs, col(1)]
                a2 = jnp.where(rmask, s[rs, col(2)], fill_right[h] if t == 0 else NEG_INF)
                m = jnp.max(jnp.maximum(jnp.maximum(a0, a1), a2), axis=-1, keepdims=True)
                e0, e1, e2 = jnp.exp2(a0 - m), jnp.exp2(a1 - m), jnp.exp2(a2 - m)
                ls.append(jnp.sum(e0 + e1 + e2, axis=-1, keepdims=True))
                p[0][h] = jnp.where(sink_left[first], 0.0, e0)
                p[1][h] = e1
                p[2][h] = jnp.where(sink_right, 0.0, e2) if t == 0 else e2
            p_rows.append(jnp.concatenate(
                [p[w][h] for w in range(3) for h in range(GROUP)], axis=1).astype(jnp.bfloat16))
            inv_rows.append(1.0 / by_head(ls))
        return jnp.concatenate(p_rows, axis=0), jnp.concatenate(inv_rows, axis=0)

    def values(u, p, inv):
        tiles = []
        for cb in window(u):
            even = vz_ref[rows_of(cb), 0:LANES]
            odd = vz_ref[rows_of(cb), LANES:2 * LANES]
            tiles += [jnp.concatenate([even, zv], axis=1), jnp.concatenate([odd, zv], axis=1),
                      jnp.concatenate([zv, even], axis=1), jnp.concatenate([zv, odd], axis=1)]
        o = jnp.dot(p, jnp.concatenate(tiles, axis=0), preferred_element_type=jnp.float32)
        o_ref[rows_of(u), :] = (o * inv).astype(o_ref.dtype)

    def pair(a, first=False, last=False):
        b = a + 1
        if not first:
            values(a - 1, p1_ref[...], inv1_ref[...])
        s_b = scores(b)
        p_a, inv_a = softmax(a, s0_ref)
        values(a, p_a, inv_a)
        if not last:
            s0_ref[...] = scores(a + 2)
        p1_ref[...], inv1_ref[...] = softmax(b, s_b)

    s0_ref[...] = scores(jnp.int32(0))
    pair(jnp.int32(0), first=True)

    def body(t, carry):
        pair(2 * t)
        return carry

    lax.fori_loop(1, nb // 2 - 1, body, 0)
    pair(jnp.int32(nb - 2), last=True)
    values(jnp.int32(nb - 1), p1_ref[...], inv1_ref[...])


def _band_attn(q, kt, vz, sink, layer, batch, seq, casts):
    assert (seq // BLOCK) % 2 == 0
    score_w = 3 * GROUP * BLOCK
    bf = jnp.bfloat16
    cast_in, cast_out, cast_shapes = [], [], []
    for w, lyr in casts:
        rows, cols = w.shape[1] // batch, w.shape[2]
        assert w.shape[1] % batch == 0 and rows % (2 * SUBLANES) == 0
        cast_in.append(pl.BlockSpec((None, rows, cols), functools.partial(lambda b, g, l: (l, b, 0), l=lyr)))
        cast_out.append(pl.BlockSpec((rows, cols), lambda b, g: (b, 0)))
        cast_shapes.append(jax.ShapeDtypeStruct(w.shape[1:], bf))
    tile = pl.BlockSpec((seq, GROUP_W), lambda b, g: (b, g))
    outs = pl.pallas_call(
        functools.partial(_band_attn_kernel, layer=layer, n_cast=len(casts)),
        out_shape=[jax.ShapeDtypeStruct(q.shape, bf)] + cast_shapes,
        grid=(batch, N_KV_HEADS),
        in_specs=[
            pl.BlockSpec(memory_space=pltpu.SMEM),
            tile,
            pl.BlockSpec((HEAD_DIM, seq), lambda b, g: (g, b)),
            tile,
        ] + cast_in,
        out_specs=[tile] + cast_out,
        scratch_shapes=[
            pltpu.VMEM((BLOCK, score_w), jnp.float32),
            pltpu.VMEM((BLOCK, score_w), jnp.bfloat16),
            pltpu.VMEM((BLOCK, GROUP_W), jnp.float32),
        ],
        compiler_params=pltpu.CompilerParams(
            dimension_semantics=("arbitrary", "arbitrary"), vmem_limit_bytes=VMEM_LIMIT),
        name="band_attn",
    )(sink, q, kt, vz, *[w for w, _ in casts])
    return outs[0], outs[1:]


def _mix_tail_kernel(x_ref, a_ref, cb_ref, u_ref, up_ref, un_ref, sga_ref, sgb_ref,
                     cw_ref, wa_ref, wc_ref, wo_ref, g_ref, o_ref, *, pos_blocks):
    i = pl.program_id(0)
    tm = x_ref.shape[0]
    u = u_ref[...].astype(jnp.float32)
    first = (i % pos_blocks) == 0
    last = (i % pos_blocks) == pos_blocks - 1
    above = jnp.where(first, 0.0, up_ref[SUBLANES - 1:SUBLANES, :].astype(jnp.float32))
    below = jnp.where(last, 0.0, un_ref[0:1, :].astype(jnp.float32))
    row8 = lax.broadcasted_iota(jnp.int32, (SUBLANES, 1), 0)
    down = pltpu.roll(u, 1, axis=0)
    up = pltpu.roll(u, tm - 1, axis=0)
    u_before = jnp.concatenate([jnp.where(row8 == 0, above, down[:SUBLANES]), down[SUBLANES:]], axis=0)
    u_after = jnp.concatenate(
        [up[:-SUBLANES], jnp.where(row8 == SUBLANES - 1, below, up[-SUBLANES:])], axis=0)
    cw = cw_ref[...]
    conv = u_before * cw[0:1, :] + u * cw[1:2, :] + u_after * cw[2:3, :]
    c = cb_ref[...] * conv.astype(jnp.bfloat16)
    for r0 in range(0, tm, ROW_SUB):
        rs = slice(r0, r0 + ROW_SUB)
        y_attn = jnp.dot(a_ref[rs, :], wa_ref[...], preferred_element_type=jnp.float32)
        y_conv = jnp.dot(c[rs], wc_ref[...], preferred_element_type=jnp.float32)
        mix = sga_ref[rs, :].astype(jnp.float32) * y_attn + sgb_ref[rs, :].astype(jnp.float32) * y_conv
        y = jnp.dot(mix.astype(jnp.bfloat16), wo_ref[...], preferred_element_type=jnp.float32)
        o_ref[rs, :] = x_ref[rs, :] + _rms_norm_f32(y, g_ref[...])


def _mix_tail(x, attn, cb, u, sga, sgb, conv_w, wa, wc, wo, g, layer, seq):
    n = x.shape[0]
    tm = MIX_TILE
    pos_blocks = seq // tm
    halo_blocks = tm // SUBLANES
    row = lambda i: (i, 0)
    prev = lambda i: (jnp.maximum(i * halo_blocks - 1, 0), 0)
    nxt = lambda i: (jnp.minimum((i + 1) * halo_blocks, n // SUBLANES - 1), 0)
    wspec = _resident((D_MODEL, D_MODEL))
    tile = pl.BlockSpec((tm, D_MODEL), row)
    return pl.pallas_call(
        functools.partial(_mix_tail_kernel, pos_blocks=pos_blocks),
        out_shape=jax.ShapeDtypeStruct(x.shape, x.dtype),
        grid=(n // tm,),
        in_specs=[
            tile, tile, tile, tile,
            pl.BlockSpec((SUBLANES, D_MODEL), prev), pl.BlockSpec((SUBLANES, D_MODEL), nxt),
            tile, tile,
            pl.BlockSpec((None, CONV_K, D_MODEL), lambda i: (layer, 0, 0)),
            wspec, wspec, wspec,
            pl.BlockSpec((None, 1, D_MODEL), lambda i: (layer, 0, 0)),
        ],
        out_specs=tile,
        compiler_params=pltpu.CompilerParams(
            dimension_semantics=("arbitrary",), vmem_limit_bytes=VMEM_LIMIT),
        name="mix_tail",
    )(x, attn, cb, u, u, u, sga, sgb, conv_w, wa, wc, wo, g)


def _ffn_kernel(x_ref, gpre_ref, wg_ref, wu_ref, wd_ref, gpost_ref, o_ref, act_ref):
    subs = [slice(r, r + ROW_SUB) for r in range(0, x_ref.shape[0], ROW_SUB)]
    hs = [_rms_norm_f32(x_ref[rs, :], gpre_ref[...]).astype(jnp.bfloat16) for rs in subs]
    for rs, h in zip(subs, hs):
        for j in range(0, D_FF, COL_CHUNK):
            w = min(COL_CHUNK, D_FF - j)
            gate = jnp.dot(h, wg_ref[:, j:j + w], preferred_element_type=jnp.float32)
            up = jnp.dot(h, wu_ref[:, j:j + w], preferred_element_type=jnp.float32)
            act_ref[rs, j:j + w] = (gate * jax.nn.sigmoid(gate) * up).astype(act_ref.dtype)
        f = jnp.dot(act_ref[rs, :], wd_ref[...], preferred_element_type=jnp.float32)
        o_ref[rs, :] = x_ref[rs, :] + _rms_norm_f32(f, gpost_ref[...])


def _ffn(x, gpre, wg, wu, wd, gpost, layer):
    n = x.shape[0]
    tm = FFN_TILE
    tile = pl.BlockSpec((tm, D_MODEL), lambda i: (i, 0))
    gspec = pl.BlockSpec((None, 1, D_MODEL), lambda i: (layer, 0, 0))
    return pl.pallas_call(
        _ffn_kernel,
        out_shape=jax.ShapeDtypeStruct(x.shape, x.dtype),
        grid=(n // tm,),
        in_specs=[
            tile, gspec,
            _resident((D_MODEL, D_FF)), _resident((D_MODEL, D_FF)), _resident((D_FF, D_MODEL)),
            gspec,
        ],
        out_specs=tile,
        scratch_shapes=[pltpu.VMEM((tm, D_FF), jnp.bfloat16)],
        compiler_params=pltpu.CompilerParams(
            dimension_semantics=("arbitrary",), vmem_limit_bytes=VMEM_LIMIT),
        name="ffn",
    )(x, gpre, wg, wu, wd, gpost)


def _rope_lane_tables(seq):
    pos = jnp.arange(seq, dtype=jnp.float32)
    inv = jnp.power(ROPE_THETA, -jnp.arange(0, ROT_DIM, 2, dtype=jnp.float32) / ROT_DIM)
    ang = pos[:, None] * inv[None, :]
    cos, sin = jnp.cos(ang), jnp.sin(ang)
    ones = jnp.ones((seq, HEAD_DIM - ROT_DIM), jnp.float32)
    zeros_r = jnp.zeros((seq, HEAD_DIM - ROT_DIM), jnp.float32)
    zeros_h = jnp.zeros((seq, ROT_HALF), jnp.float32)
    c = jnp.concatenate([cos, cos, ones], axis=1)
    sa = jnp.concatenate([-sin, zeros_h, zeros_r], axis=1)
    sb = jnp.concatenate([zeros_h, sin, zeros_r], axis=1)
    rep = LANES // HEAD_DIM
    return tuple(jnp.tile(t, (1, rep)) for t in (c, sa, sb)), (cos.T, sin.T)


def kernel(x, g_pre_mix, w_in, attn_sink, conv_w, w_attn_proj, w_conv_proj, w_out,
           g_post_mix, g_pre_ffn, w_gate, w_up, w_down, g_post_ffn):
    batch, seq, d = x.shape
    depth = w_in.shape[0]
    assert d == D_MODEL and seq % max(IN_TILE, MIX_TILE, FFN_TILE) == 0 and seq % BLOCK == 0
    gains = [g.reshape(depth, 1, D_MODEL) for g in (g_pre_mix, g_post_mix, g_pre_ffn, g_post_ffn)]
    tabs, tabs_t = _rope_lane_tables(seq)
    xf = x.reshape(batch * seq, d)
    w_in_bf = w_in[0].astype(jnp.bfloat16)
    for layer in range(depth):
        q, kt, vz, cb, u, sga, sgb = _in_proj(xf, gains[0], w_in_bf, layer, tabs, tabs_t, seq)
        casts = [(w, layer) for w in (w_attn_proj, w_conv_proj, w_out, w_gate, w_up, w_down)]
        if layer + 1 < depth:
            casts.append((w_in, layer + 1))
        attn, cast = _band_attn(q, kt, vz, attn_sink, layer, batch, seq, casts)
        wa, wc, wo, wg, wu, wd = cast[:6]
        if layer + 1 < depth:
            w_in_bf = cast[6]
        xf = _mix_tail(xf, attn, cb, u, sga, sgb, conv_w, wa, wc, wo, gains[1], layer, seq)
        xf = _ffn(xf, gains[2], wg, wu, wd, gains[3], layer)
    return xf.reshape(batch, seq, d)
```

```python
import functools

import jax
import jax.numpy as jnp
from jax import lax
from jax.experimental import pallas as pl
from jax.experimental.pallas import tpu as pltpu

D_MODEL = 1024
N_HEADS = 16
HEAD_DIM = 64
N_KV_HEADS = 4
GROUP = N_HEADS // N_KV_HEADS
ROT_DIM = HEAD_DIM // 4
ROT_HALF = ROT_DIM // 2
ROPE_THETA = 500000.0
WINDOW = 128
BLOCK = 128
CONV_K = 3
D_FF = 2816
Q_W = N_HEADS * HEAD_DIM
KV_W = N_KV_HEADS * HEAD_DIM
GROUP_W = GROUP * HEAD_DIM
IN_COLS = Q_W + 2 * KV_W + 3 * D_MODEL + 2 * D_MODEL
RMS_EPS = 1e-6
NEG_INF = -1e30
LOG2E = 1.4426950408889634

LANES = 128
SUBLANES = 8
IN_TILE = 1024
MIX_TILE = 1024
FFN_TILE = 1024
COL_CHUNK = 512
ROW_SUB = 256
SLAB = 16
VMEM_LIMIT = 56 * 1024 * 1024
RESIDENT = pl.Buffered(1)

_OFF_Q = 0
_OFF_K = Q_W
_OFF_V = Q_W + KV_W
_OFF_CB = Q_W + 2 * KV_W
_OFF_CC = _OFF_CB + D_MODEL
_OFF_CX = _OFF_CC + D_MODEL
_OFF_GA = _OFF_CX + D_MODEL
_OFF_GB = _OFF_GA + D_MODEL


def _rms_norm_f32(x, g):
    return x * lax.rsqrt(jnp.mean(x * x, axis=-1, keepdims=True) + RMS_EPS) * g


def _rope_chunk(z, c, sa, sb):
    return z * c + pltpu.roll(z, LANES - ROT_HALF, axis=1) * sa + pltpu.roll(z, ROT_HALF, axis=1) * sb


def _resident(shape):
    return pl.BlockSpec(shape, lambda *_: (0,) * len(shape), pipeline_mode=RESIDENT)


def _in_proj_kernel(x_ref, g_ref, w_ref, c_ref, sa_ref, sb_ref, ct_ref, st_ref,
                    q_ref, kt_ref, vz_ref, cb_ref, u_ref, sga_ref, sgb_ref):
    scale = HEAD_DIM ** -0.5 * LOG2E
    subs = [slice(r, r + ROW_SUB) for r in range(0, x_ref.shape[0], ROW_SUB)]
    hs = [_rms_norm_f32(x_ref[rs, :], g_ref[...]).astype(jnp.bfloat16) for rs in subs]
    for rs, h in zip(subs, hs):
        def proj(off, width):
            return jnp.dot(h, w_ref[:, off:off + width], preferred_element_type=jnp.float32)

        c, sa, sb = c_ref[rs, :], sa_ref[rs, :], sb_ref[rs, :]

        def rope(z):
            return jnp.concatenate(
                [_rope_chunk(z[:, j:j + LANES], c, sa, sb) for j in range(0, z.shape[1], LANES)], axis=1)

        for j in range(0, Q_W, COL_CHUNK):
            q_ref[rs, j:j + COL_CHUNK] = (rope(proj(_OFF_Q + j, COL_CHUNK)) * scale).astype(q_ref.dtype)

        kt = proj(_OFF_K, KV_W).T
        ct, st = ct_ref[:, rs], st_ref[:, rs]
        pieces = []
        for g in range(N_KV_HEADS):
            base = g * HEAD_DIM
            x1 = kt[base:base + ROT_HALF]
            x2 = kt[base + ROT_HALF:base + ROT_DIM]
            pieces += [x1 * ct - x2 * st, x2 * ct + x1 * st, kt[base + ROT_DIM:base + HEAD_DIM]]
        kt_ref[:, rs] = jnp.concatenate(pieces, axis=0).astype(kt_ref.dtype)

        v = proj(_OFF_V, KV_W)
        zero = jnp.zeros((v.shape[0], HEAD_DIM), jnp.float32)
        halves = []
        for g in range(N_KV_HEADS):
            vg = v[:, g * HEAD_DIM:(g + 1) * HEAD_DIM]
            halves += [vg, zero, zero, vg]
        vz_ref[rs, :] = jnp.concatenate(halves, axis=1).astype(vz_ref.dtype)

        for j in range(0, D_MODEL, COL_CHUNK):
            cols = slice(j, j + COL_CHUNK)
            cb_ref[rs, cols] = proj(_OFF_CB + j, COL_CHUNK).astype(cb_ref.dtype)
            u = proj(_OFF_CC + j, COL_CHUNK) * proj(_OFF_CX + j, COL_CHUNK)
            u_ref[rs, cols] = u.astype(u_ref.dtype)
            sga_ref[rs, cols] = jax.nn.sigmoid(proj(_OFF_GA + j, COL_CHUNK)).astype(sga_ref.dtype)
            sgb_ref[rs, cols] = jax.nn.sigmoid(proj(_OFF_GB + j, COL_CHUNK)).astype(sgb_ref.dtype)


def _in_proj(x, g, w_in, layer, tabs, tabs_t, seq):
    n = x.shape[0]
    tm = IN_TILE
    pos_blocks = seq // tm
    row = lambda i: (i, 0)
    tab = lambda i: (i % pos_blocks, 0)
    tab_t = lambda i: (0, i % pos_blocks)
    bf = jnp.bfloat16
    out_shape = (
        jax.ShapeDtypeStruct((n, Q_W), bf), jax.ShapeDtypeStruct((KV_W, n), bf),
        jax.ShapeDtypeStruct((n, N_KV_HEADS * GROUP_W), bf), jax.ShapeDtypeStruct((n, D_MODEL), bf),
        jax.ShapeDtypeStruct((n, D_MODEL), bf), jax.ShapeDtypeStruct((n, D_MODEL), bf),
        jax.ShapeDtypeStruct((n, D_MODEL), bf))
    return pl.pallas_call(
        _in_proj_kernel,
        out_shape=out_shape,
        grid=(n // tm,),
        in_specs=[
            pl.BlockSpec((tm, D_MODEL), row),
            pl.BlockSpec((None, 1, D_MODEL), lambda i: (layer, 0, 0)),
            _resident((D_MODEL, IN_COLS)),
            pl.BlockSpec((tm, LANES), tab), pl.BlockSpec((tm, LANES), tab), pl.BlockSpec((tm, LANES), tab),
            pl.BlockSpec((ROT_HALF, tm), tab_t), pl.BlockSpec((ROT_HALF, tm), tab_t),
        ],
        out_specs=(
            pl.BlockSpec((tm, Q_W), row), pl.BlockSpec((KV_W, tm), lambda i: (0, i)),
            pl.BlockSpec((tm, N_KV_HEADS * GROUP_W), row),
            pl.BlockSpec((tm, D_MODEL), row), pl.BlockSpec((tm, D_MODEL), row),
            pl.BlockSpec((tm, D_MODEL), row), pl.BlockSpec((tm, D_MODEL), row)),
        compiler_params=pltpu.CompilerParams(
            dimension_semantics=("arbitrary",), vmem_limit_bytes=VMEM_LIMIT),
        name="in_proj",
    )(x, g, w_in, *tabs, *tabs_t)


def _band_attn_kernel(sink_ref, q_ref, kt_ref, vz_ref, *rest, layer, n_cast):
    cast_src = rest[:n_cast]
    o_ref = rest[n_cast]
    cast_dst = rest[n_cast + 1:2 * n_cast + 1]
    s0_ref, p1_ref, inv1_ref = rest[2 * n_cast + 1:]
    g = pl.program_id(1)
    nb = q_ref.shape[0] // BLOCK
    n_slabs = BLOCK // SLAB
    r = lax.broadcasted_iota(jnp.int32, (SLAB, BLOCK), 0)
    c = lax.broadcasted_iota(jnp.int32, (SLAB, BLOCK), 1)
    left_ok = [c >= r + t * SLAB for t in range(n_slabs)]
    right_ok = [c <= r + t * SLAB for t in range(n_slabs)]
    head_slot = lax.broadcasted_iota(jnp.int32, (SLAB, GROUP_W), 1) // HEAD_DIM
    sinks = [sink_ref[layer, g * GROUP + h] * LOG2E for h in range(GROUP)]
    zk = jnp.zeros((HEAD_DIM, BLOCK), jnp.bfloat16)
    zv = jnp.zeros((BLOCK, LANES), jnp.bfloat16)

    def by_head(cols):
        return jnp.where(head_slot == 0, cols[0],
                         jnp.where(head_slot == 1, cols[1], jnp.where(head_slot == 2, cols[2], cols[3])))

    sink_left = [(c == 0) & (r >= 1), c == 0]
    sink_right = (c == BLOCK - 1) & (r == 0)
    fill_left = [[jnp.where(msk, sinks[h], NEG_INF) for h in range(GROUP)] for msk in sink_left]
    fill_right = [jnp.where(sink_right, sinks[h], NEG_INF) for h in range(GROUP)]

    def rows_of(u):
        return pl.ds(pl.multiple_of(u * BLOCK, BLOCK), BLOCK)

    def window(u):
        return (jnp.maximum(u - 1, 0), u, jnp.minimum(u + 1, nb - 1))

    def scores(u):
        tiles = []
        for cb in window(u):
            kt = kt_ref[:, rows_of(cb)]
            tiles.append(jnp.concatenate(
                [jnp.concatenate([kt if a == h else zk for a in range(GROUP)], axis=1) for h in range(GROUP)],
                axis=0))
        return jnp.dot(q_ref[rows_of(u), :], jnp.concatenate(tiles, axis=1), preferred_element_type=jnp.float32)

    def softmax(u, s):
        has_left, has_right = u > 0, u < nb - 1
        p_rows, inv_rows = [], []
        for t in range(n_slabs):
            rs = slice(t * SLAB, (t + 1) * SLAB)
            lmask = left_ok[t] & has_left
            rmask = right_ok[t] & has_right
            p = [[None] * GROUP for _ in range(3)]
            ls = []
            first = min(t, 1)
            for h in range(GROUP):
                col = lambda w: slice((w * GROUP + h) * BLOCK, (w * GROUP + h + 1) * BLOCK)
                a0 = jnp.where(lmask, s[rs, col(0)], fill_left[first][h])
                a1 = s[rs, col(1)]
                a2 = jnp.where(rmask, s[rs, col(2)], fill_right[h] if t == 0 else NEG_INF)
                m = jnp.max(jnp.maximum(jnp.maximum(a0, a1), a2), axis=-1, keepdims=True)
                e0, e1, e2 = jnp.exp2(a0 - m), jnp.exp2(a1 - m), jnp.exp2(a2 - m)
                ls.append(jnp.sum(e0 + e1 + e2, axis=-1, keepdims=True))
                p[0][h] = jnp.where(sink_left[first], 0.0, e0)
                p[1][h] = e1
                p[2][h] = jnp.where(sink_right, 0.0, e2) if t == 0 else e2
            p_rows.append(jnp.concatenate(
                [p[w][h] for w in range(3) for h in range(GROUP)], axis=1).astype(jnp.bfloat16))
            inv_rows.append(1.0 / by_head(ls))
        return jnp.concatenate(p_rows, axis=0), jnp.concatenate(inv_rows, axis=0)

    def values(u, p, inv):
        tiles = []
        for cb in window(u):
            even = vz_ref[rows_of(cb), 0:LANES]
            odd = vz_ref[rows_of(cb), LANES:2 * LANES]
            tiles += [jnp.concatenate([even, zv], axis=1), jnp.concatenate([odd, zv], axis=1),
                      jnp.concatenate([zv, even], axis=1), jnp.concatenate([zv, odd], axis=1)]
        o = jnp.dot(p, jnp.concatenate(tiles, axis=0), preferred_element_type=jnp.float32)
        o_ref[rows_of(u), :] = (o * inv).astype(o_ref.dtype)

    def pair(a, first=False, last=False):
        b = a + 1
        if not first:
            values(a - 1, p1_ref[...], inv1_ref[...])
        s_b = scores(b)
        p_a, inv_a = softmax(a, s0_ref)
        values(a, p_a, inv_a)
        if not last:
            s0_ref[...] = scores(a + 2)
        p1_ref[...], inv1_ref[...] = softmax(b, s_b)

    s0_ref[...] = scores(jnp.int32(0))
    pair(jnp.int32(0), first=True)

    def body(t, carry):
        pair(2 * t)
        return carry

    lax.fori_loop(1, nb // 2 - 1, body, 0)
    pair(jnp.int32(nb - 2), last=True)
    values(jnp.int32(nb - 1), p1_ref[...], inv1_ref[...])

    @pl.when(g == N_KV_HEADS - 1)
    def _():
        for src, dst in zip(cast_src, cast_dst):
            dst[...] = src[...].astype(dst.dtype)


def _band_attn(q, kt, vz, sink, layer, batch, seq, casts):
    assert (seq // BLOCK) % 2 == 0
    score_w = 3 * GROUP * BLOCK
    bf = jnp.bfloat16
    cast_in, cast_out, cast_shapes = [], [], []
    for w, lyr in casts:
        rows, cols = w.shape[1] // batch, w.shape[2]
        assert w.shape[1] % batch == 0 and rows % (2 * SUBLANES) == 0
        cast_in.append(pl.BlockSpec((None, rows, cols), functools.partial(lambda b, g, l: (l, b, 0), l=lyr)))
        cast_out.append(pl.BlockSpec((rows, cols), lambda b, g: (b, 0)))
        cast_shapes.append(jax.ShapeDtypeStruct(w.shape[1:], bf))
    tile = pl.BlockSpec((seq, GROUP_W), lambda b, g: (b, g))
    outs = pl.pallas_call(
        functools.partial(_band_attn_kernel, layer=layer, n_cast=len(casts)),
        out_shape=[jax.ShapeDtypeStruct(q.shape, bf)] + cast_shapes,
        grid=(batch, N_KV_HEADS),
        in_specs=[
            pl.BlockSpec(memory_space=pltpu.SMEM),
            tile,
            pl.BlockSpec((HEAD_DIM, seq), lambda b, g: (g, b)),
            tile,
        ] + cast_in,
        out_specs=[tile] + cast_out,
        scratch_shapes=[
            pltpu.VMEM((BLOCK, score_w), jnp.float32),
            pltpu.VMEM((BLOCK, score_w), jnp.bfloat16),
            pltpu.VMEM((BLOCK, GROUP_W), jnp.float32),
        ],
        compiler_params=pltpu.CompilerParams(
            dimension_semantics=("arbitrary", "arbitrary"), vmem_limit_bytes=VMEM_LIMIT),
        name="band_attn",
    )(sink, q, kt, vz, *[w for w, _ in casts])
    return outs[0], outs[1:]


def _mix_tail_kernel(x_ref, a_ref, cb_ref, u_ref, up_ref, un_ref, sga_ref, sgb_ref,
                     cw_ref, wa_ref, wc_ref, wo_ref, g_ref, o_ref, *, pos_blocks):
    i = pl.program_id(0)
    tm = x_ref.shape[0]
    u = u_ref[...].astype(jnp.float32)
    first = (i % pos_blocks) == 0
    last = (i % pos_blocks) == pos_blocks - 1
    above = jnp.where(first, 0.0, up_ref[SUBLANES - 1:SUBLANES, :].astype(jnp.float32))
    below = jnp.where(last, 0.0, un_ref[0:1, :].astype(jnp.float32))
    row8 = lax.broadcasted_iota(jnp.int32, (SUBLANES, 1), 0)
    down = pltpu.roll(u, 1, axis=0)
    up = pltpu.roll(u, tm - 1, axis=0)
    u_before = jnp.concatenate([jnp.where(row8 == 0, above, down[:SUBLANES]), down[SUBLANES:]], axis=0)
    u_after = jnp.concatenate(
        [up[:-SUBLANES], jnp.where(row8 == SUBLANES - 1, below, up[-SUBLANES:])], axis=0)
    cw = cw_ref[...]
    conv = u_before * cw[0:1, :] + u * cw[1:2, :] + u_after * cw[2:3, :]
    c = cb_ref[...] * conv.astype(jnp.bfloat16)
    for r0 in range(0, tm, ROW_SUB):
        rs = slice(r0, r0 + ROW_SUB)
        y_attn = jnp.dot(a_ref[rs, :], wa_ref[...], preferred_element_type=jnp.float32)
        y_conv = jnp.dot(c[rs], wc_ref[...], preferred_element_type=jnp.float32)
        mix = sga_ref[rs, :].astype(jnp.float32) * y_attn + sgb_ref[rs, :].astype(jnp.float32) * y_conv
        y = jnp.dot(mix.astype(jnp.bfloat16), wo_ref[...], preferred_element_type=jnp.float32)
        o_ref[rs, :] = x_ref[rs, :] + _rms_norm_f32(y, g_ref[...])


def _mix_tail(x, attn, cb, u, sga, sgb, conv_w, wa, wc, wo, g, layer, seq):
    n = x.shape[0]
    tm = MIX_TILE
    pos_blocks = seq // tm
    halo_blocks = tm // SUBLANES
    row = lambda i: (i, 0)
    prev = lambda i: (jnp.maximum(i * halo_blocks - 1, 0), 0)
    nxt = lambda i: (jnp.minimum((i + 1) * halo_blocks, n // SUBLANES - 1), 0)
    wspec = _resident((D_MODEL, D_MODEL))
    tile = pl.BlockSpec((tm, D_MODEL), row)
    return pl.pallas_call(
        functools.partial(_mix_tail_kernel, pos_blocks=pos_blocks),
        out_shape=jax.ShapeDtypeStruct(x.shape, x.dtype),
        grid=(n // tm,),
        in_specs=[
            tile, tile, tile, tile,
            pl.BlockSpec((SUBLANES, D_MODEL), prev), pl.BlockSpec((SUBLANES, D_MODEL), nxt),
            tile, tile,
            pl.BlockSpec((None, CONV_K, D_MODEL), lambda i: (layer, 0, 0)),
            wspec, wspec, wspec,
            pl.BlockSpec((None, 1, D_MODEL), lambda i: (layer, 0, 0)),
        ],
        out_specs=tile,
        compiler_params=pltpu.CompilerParams(
            dimension_semantics=("arbitrary",), vmem_limit_bytes=VMEM_LIMIT),
        name="mix_tail",
    )(x, attn, cb, u, u, u, sga, sgb, conv_w, wa, wc, wo, g)


def _ffn_kernel(x_ref, gpre_ref, wg_ref, wu_ref, wd_ref, gpost_ref, o_ref, act_ref):
    subs = [slice(r, r + ROW_SUB) for r in range(0, x_ref.shape[0], ROW_SUB)]
    hs = [_rms_norm_f32(x_ref[rs, :], gpre_ref[...]).astype(jnp.bfloat16) for rs in subs]
    for rs, h in zip(subs, hs):
        for j in range(0, D_FF, COL_CHUNK):
            w = min(COL_CHUNK, D_FF - j)
            gate = jnp.dot(h, wg_ref[:, j:j + w], preferred_element_type=jnp.float32)
            up = jnp.dot(h, wu_ref[:, j:j + w], preferred_element_type=jnp.float32)
            act_ref[rs, j:j + w] = (gate * jax.nn.sigmoid(gate) * up).astype(act_ref.dtype)
        f = jnp.dot(act_ref[rs, :], wd_ref[...], preferred_element_type=jnp.float32)
        o_ref[rs, :] = x_ref[rs, :] + _rms_norm_f32(f, gpost_ref[...])


def _ffn(x, gpre, wg, wu, wd, gpost, layer):
    n = x.shape[0]
    tm = FFN_TILE
    tile = pl.BlockSpec((tm, D_MODEL), lambda i: (i, 0))
    gspec = pl.BlockSpec((None, 1, D_MODEL), lambda i: (layer, 0, 0))
    return pl.pallas_call(
        _ffn_kernel,
        out_shape=jax.ShapeDtypeStruct(x.shape, x.dtype),
        grid=(n // tm,),
        in_specs=[
            tile, gspec,
            _resident((D_MODEL, D_FF)), _resident((D_MODEL, D_FF)), _resident((D_FF, D_MODEL)),
            gspec,
        ],
        out_specs=tile,
        scratch_shapes=[pltpu.VMEM((tm, D_FF), jnp.bfloat16)],
        compiler_params=pltpu.CompilerParams(
            dimension_semantics=("arbitrary",), vmem_limit_bytes=VMEM_LIMIT),
        name="ffn",
    )(x, gpre, wg, wu, wd, gpost)


def _rope_lane_tables(seq):
    pos = jnp.arange(seq, dtype=jnp.float32)
    inv = jnp.power(ROPE_THETA, -jnp.arange(0, ROT_DIM, 2, dtype=jnp.float32) / ROT_DIM)
    ang = pos[:, None] * inv[None, :]
    cos, sin = jnp.cos(ang), jnp.sin(ang)
    ones = jnp.ones((seq, HEAD_DIM - ROT_DIM), jnp.float32)
    zeros_r = jnp.zeros((seq, HEAD_DIM - ROT_DIM), jnp.float32)
    zeros_h = jnp.zeros((seq, ROT_HALF), jnp.float32)
    c = jnp.concatenate([cos, cos, ones], axis=1)
    sa = jnp.concatenate([-sin, zeros_h, zeros_r], axis=1)
    sb = jnp.concatenate([zeros_h, sin, zeros_r], axis=1)
    rep = LANES // HEAD_DIM
    return tuple(jnp.tile(t, (1, rep)) for t in (c, sa, sb)), (cos.T, sin.T)


def kernel(x, g_pre_mix, w_in, attn_sink, conv_w, w_attn_proj, w_conv_proj, w_out,
           g_post_mix, g_pre_ffn, w_gate, w_up, w_down, g_post_ffn):
    batch, seq, d = x.shape
    depth = w_in.shape[0]
    assert d == D_MODEL and seq % max(IN_TILE, MIX_TILE, FFN_TILE) == 0 and seq % BLOCK == 0
    gains = [g.reshape(depth, 1, D_MODEL) for g in (g_pre_mix, g_post_mix, g_pre_ffn, g_post_ffn)]
    tabs, tabs_t = _rope_lane_tables(seq)
    xf = x.reshape(batch * seq, d)
    w_in_bf = w_in[0].astype(jnp.bfloat16)
    for layer in range(depth):
        q, kt, vz, cb, u, sga, sgb = _in_proj(xf, gains[0], w_in_bf, layer, tabs, tabs_t, seq)
        casts = [(w, layer) for w in (w_attn_proj, w_conv_proj, w_out, w_gate, w_up, w_down)]
        if layer + 1 < depth:
            casts.append((w_in, layer + 1))
        attn, cast = _band_attn(q, kt, vz, attn_sink, layer, batch, seq, casts)
        wa, wc, wo, wg, wu, wd = cast[:6]
        if layer + 1 < depth:
            w_in_bf = cast[6]
        xf = _mix_tail(xf, attn, cb, u, sga, sgb, conv_w, wa, wc, wo, gains[1], layer, seq)
        xf = _ffn(xf, gains[2], wg, wu, wd, gains[3], layer)
    return xf.reshape(batch, seq, d)
```
